```python
import math
import jax
import jax.numpy as jnp
from jax import lax
import numpy as np

D_MODEL = 1024
BATCH = 2
SEQ = 8192
DEPTH = 2
DEC_BATCH = 128
DEC_SEQ = 1
PAST_LEN = 2048
PAGE_SIZE = 128

N_A_LAYERS = DEPTH // 2
N_B_LAYERS = DEPTH - N_A_LAYERS

EXPAND = 2
D_INNER = EXPAND * D_MODEL
HEADDIM_A = 64
N_HEADS_A = D_INNER // HEADDIM_A
D_STATE = 128
N_GROUPS_A = 8
HEADS_PER_GROUP = N_HEADS_A // N_GROUPS_A
CONV_W = 4
CONV_DIM = D_INNER + 2 * N_GROUPS_A * D_STATE
IN_PROJ_A = D_INNER + CONV_DIM + N_HEADS_A
SSD_CHUNK = 128
DT_MIN = 0.001
DT_MAX = 0.1

HEAD_DIM_B = 64
N_HEADS_B = D_MODEL // (2 * HEAD_DIM_B)
QK_WIDTH_B = N_HEADS_B * 2 * HEAD_DIM_B
V_WIDTH_B = N_HEADS_B * 2 * HEAD_DIM_B
Q_BLOCK = 128
ROPE_THETA = 10000.0

D_FF = 4 * D_MODEL
DN_ALPHA = (2 * DEPTH) ** 0.25
DN_BETA = (8 * DEPTH) ** -0.25
LN_EPS = 1e-5
RMS_EPS = 1e-5

kernel_name = 'yoco_mamba2_diffattn_step'


def layer_norm(x, g, b):
    xf = x.astype(jnp.float32)
    mu = jnp.mean(xf, -1, keepdims=True)
    var = jnp.mean(jnp.square(xf - mu), -1, keepdims=True)
    return ((xf - mu) * lax.rsqrt(var + LN_EPS) * g + b).astype(x.dtype)


def rms_norm(x, g):
    xf = x.astype(jnp.float32)
    return (xf * lax.rsqrt(jnp.mean(xf * xf, -1, keepdims=True) + RMS_EPS) * g).astype(x.dtype)


def lambda_init_fn(layer):
    return 0.8 - 0.6 * math.exp(-0.3 * layer)


def rope(x, pos):
    half = HEAD_DIM_B // 2
    inv = ROPE_THETA ** (-jnp.arange(half, dtype=jnp.float32) / half)
    ang = pos.astype(jnp.float32)[:, None] * inv[None, :]
    cos = jnp.cos(ang)[None, :, None, None, :]
    sin = jnp.sin(ang)[None, :, None, None, :]
    xf = x.astype(jnp.float32)
    x1, x2 = xf[..., :half], xf[..., half:]
    return jnp.concatenate([x1 * cos - x2 * sin, x2 * cos + x1 * sin], -1).astype(x.dtype)


def causal_conv(xbc, prefix, w, b):
    xpad = jnp.concatenate([prefix.astype(xbc.dtype), xbc], axis=1)
    y = lax.conv_general_dilated(xpad, w[:, None, :].astype(xbc.dtype), window_strides=(1,),
                                 padding='VALID', dimension_numbers=('NWC', 'WIO', 'NWC'),
                                 feature_group_count=CONV_DIM)
    return y + b, xpad[:, -(CONV_W - 1):]


def ssd_scan(x, dt, a, bmat, cmat, h0):
    f32 = jnp.float32
    bt, l = x.shape[:2]
    q = math.gcd(l, SSD_CHUNK)
    nc = l // q
    G, R, P, N = N_GROUPS_A, HEADS_PER_GROUP, HEADDIM_A, D_STATE
    xs = (x.astype(f32) * dt[..., None]).reshape(bt, nc, q, G, R, P)
    da = jnp.moveaxis((dt * a).reshape(bt, nc, q, G, R), 2, -1)
    cs = jnp.cumsum(da, axis=-1)
    bm = bmat.astype(f32).reshape(bt, nc, q, G, N)
    cm = cmat.astype(f32).reshape(bt, nc, q, G, N)
    idx = jnp.arange(q)
    causal = idx[:, None] >= idx[None, :]
    decay = jnp.exp(jnp.where(causal, cs[..., :, None] - cs[..., None, :], -jnp.inf))
    cb = jnp.einsum('bclgn,bcsgn->bcgls', cm, bm)
    y_diag = jnp.einsum('bcgls,bcgrls,bcsgrp->bclgrp', cb, decay, xs)
    decay_to_end = jnp.exp(cs[..., -1:] - cs)
    chunk_states = jnp.einsum('bcsgn,bcgrs,bcsgrp->bcgrpn', bm, decay_to_end, xs)
    chunk_decay = jnp.exp(cs[..., -1])

    def step(h, inp):
        st, dec = inp
        return h * dec[..., None, None] + st, h

    h_final, h_prev = lax.scan(step, h0.astype(f32).reshape(bt, G, R, P, N),
                               (jnp.moveaxis(chunk_states, 1, 0), jnp.moveaxis(chunk_decay, 1, 0)))
    h_prev = jnp.moveaxis(h_prev, 0, 1)
    y_off = jnp.einsum('bclgn,bcgrpn,bcgrl->bclgrp', cm, h_prev, jnp.exp(cs))
    y = (y_diag + y_off).reshape(bt, l, N_HEADS_A, P)
    return y, h_final.reshape(bt, N_HEADS_A, P, N)


def mamba2_mixer(x, conv_prefix, h0, w_in, conv_w, conv_b, dt_bias, a_log, d_skip, norm_g, w_out):
    f32 = jnp.float32
    bt, l, _ = x.shape
    gn = N_GROUPS_A * D_STATE
    zxbcdt = x @ w_in
    z = zxbcdt[..., :D_INNER]
    xbc = zxbcdt[..., D_INNER:D_INNER + CONV_DIM]
    dt_raw = zxbcdt[..., D_INNER + CONV_DIM:]
    xbc, conv_state = causal_conv(xbc, conv_prefix, conv_w, conv_b)
    xbc = jax.nn.silu(xbc)
    xh = xbc[..., :D_INNER].reshape(bt, l, N_HEADS_A, HEADDIM_A)
    bmat = xbc[..., D_INNER:D_INNER + gn].reshape(bt, l, N_GROUPS_A, D_STATE)
    cmat = xbc[..., D_INNER + gn:].reshape(bt, l, N_GROUPS_A, D_STATE)
    dt = jax.nn.softplus(dt_raw.astype(f32) + dt_bias.astype(f32))
    a = -jnp.exp(a_log.astype(f32))
    y, h_final = ssd_scan(xh, dt, a, bmat, cmat, h0)
    y = y + d_skip.astype(f32)[:, None] * xh.astype(f32)
    y = y.reshape(bt, l, D_INNER) * jax.nn.silu(z.astype(f32))
    yg = y.reshape(bt, l, N_GROUPS_A, D_INNER // N_GROUPS_A)
    yg = yg * lax.rsqrt(jnp.mean(yg * yg, -1, keepdims=True) + RMS_EPS)
    y = (yg.reshape(bt, l, D_INNER) * norm_g).astype(x.dtype)
    return y @ w_out, conv_state, h_final.astype(x.dtype)


def shared_kv(h, w_kv, pos):
    bt, l, _ = h.shape
    kv = h @ w_kv
    k = rope(kv[..., :QK_WIDTH_B].reshape(bt, l, N_HEADS_B, 2, HEAD_DIM_B), pos)
    v = kv[..., QK_WIDTH_B:].reshape(bt, l, N_HEADS_B, 2 * HEAD_DIM_B)
    return k, v


def diff_attn_prompt(q, k, v, lam):
    b, s = q.shape[:2]
    nblk = s // Q_BLOCK
    scale = HEAD_DIM_B ** -0.5
    qb = jnp.moveaxis(q.reshape(b, nblk, Q_BLOCK, N_HEADS_B, 2, HEAD_DIM_B), 1, 0)
    kpos = jnp.arange(s)

    def block(args):
        qi, i = args
        sc = jnp.einsum('bqhmd,bkhmd->bhmqk', qi, k).astype(jnp.float32) * scale
        qpos = i * Q_BLOCK + jnp.arange(Q_BLOCK)
        sc = jnp.where(kpos[None, :] <= qpos[:, None], sc, -jnp.inf)
        p = jax.nn.softmax(sc, axis=-1)
        w = (p[:, :, 0] - lam * p[:, :, 1]).astype(v.dtype)
        return jnp.einsum('bhqk,bkhe->bqhe', w, v)

    out = lax.map(block, (qb, jnp.arange(nblk)))
    return jnp.moveaxis(out, 0, 1).reshape(b, s, N_HEADS_B, 2 * HEAD_DIM_B)


def diff_attn_sample(q, k_new, v_new, k_past, v_past, lam):
    t = q.shape[1]
    lp = k_past.shape[1]
    scale = HEAD_DIM_B ** -0.5
    sc_past = jnp.einsum('bqhmd,bkhmd->bhmqk', q, k_past).astype(jnp.float32)
    sc_new = jnp.einsum('bqhmd,bkhmd->bhmqk', q, k_new).astype(jnp.float32)
    tri = jnp.arange(t)[:, None] >= jnp.arange(t)[None, :]
    sc_new = jnp.where(tri, sc_new, -jnp.inf)
    p = jax.nn.softmax(jnp.concatenate([sc_past, sc_new], -1) * scale, axis=-1)
    w = (p[:, :, 0] - lam * p[:, :, 1]).astype(v_new.dtype)
    return (jnp.einsum('bhqk,bkhe->bqhe', w[..., :lp], v_past)
            + jnp.einsum('bhqk,bkhe->bqhe', w[..., lp:], v_new))


def diff_attention(x, k, v, k_past, v_past, pos, lam_init, w_q, lq1, lk1, lq2, lk2, subln_g, w_o):
    f32 = jnp.float32
    bt, l, _ = x.shape
    q = rope((x @ w_q).reshape(bt, l, N_HEADS_B, 2, HEAD_DIM_B), pos)
    lam = (jnp.exp(jnp.sum(lq1.astype(f32) * lk1.astype(f32)))
           - jnp.exp(jnp.sum(lq2.astype(f32) * lk2.astype(f32))) + lam_init)
    if k_past is None:
        o = diff_attn_prompt(q, k, v, lam)
    else:
        o = diff_attn_sample(q, k, v, k_past, v_past, lam)
    o = rms_norm(o, subln_g) * (1.0 - lam_init)
    return o.reshape(bt, l, V_WIDTH_B) @ w_o


def sq_relu_mlp(x, w_up, w_down):
    return jnp.square(jax.nn.relu(x @ w_up)) @ w_down


def trunk(x, pos, conv_init, ssm_init, k_past, v_past, p):
    conv_out, ssm_out = [], []
    k_sh, v_sh = None, None
    for layer in range(DEPTH):
        if layer < N_A_LAYERS:
            i = layer
            y, cst, hst = mamba2_mixer(x, conv_init[i], ssm_init[i], p['w_in_a'][i], p['conv_w_a'][i],
                                       p['conv_b_a'][i], p['dt_bias_a'][i], p['a_log_a'][i],
                                       p['d_skip_a'][i], p['norm_a'][i], p['w_out_a'][i])
            conv_out.append(cst)
            ssm_out.append(hst)
        else:
            if layer == N_A_LAYERS:
                k_sh, v_sh = shared_kv(x, p['w_kv'], pos)
            j = layer - N_A_LAYERS
            y = diff_attention(x, k_sh, v_sh, k_past, v_past, pos, lambda_init_fn(layer),
                               p['w_q_b'][j], p['lambda_q1_b'][j], p['lambda_k1_b'][j],
                               p['lambda_q2_b'][j], p['lambda_k2_b'][j], p['subln_b'][j], p['w_o_b'][j])
        x = layer_norm(DN_ALPHA * x + y, p['ln_g'][layer, 0], p['ln_b'][layer, 0])
        x = layer_norm(DN_ALPHA * x + sq_relu_mlp(x, p['w_up'][layer], p['w_down'][layer]),
                       p['ln_g'][layer, 1], p['ln_b'][layer, 1])
    return x, jnp.stack(conv_out), jnp.stack(ssm_out), k_sh, v_sh


def setup_inputs(seed: int = 0) -> dict:
    key = jax.random.key(seed)
    ks = jax.random.split(key, 32)
    f32 = jnp.float32
    n_pages = PAST_LEN // PAGE_SIZE
    n_phys = (DEC_BATCH * n_pages * 5) // 4

    def nrm(k, shape, scale):
        return jax.random.normal(k, shape, f32) * scale

    x_prompt = nrm(ks[0], (BATCH, SEQ, D_MODEL), 1.0)
    x_sample = nrm(ks[1], (DEC_BATCH, DEC_SEQ, D_MODEL), 1.0)
    state_conv = nrm(ks[2], (N_A_LAYERS, DEC_BATCH, CONV_W - 1, CONV_DIM), 1.0)
    state_ssm = nrm(ks[3], (N_A_LAYERS, DEC_BATCH, N_HEADS_A, HEADDIM_A, D_STATE), 0.1)
    cache_k = nrm(ks[4], (n_phys, PAGE_SIZE, N_HEADS_B, 2, HEAD_DIM_B), 1.0)
    cache_v = nrm(ks[5], (n_phys, PAGE_SIZE, N_HEADS_B, 2 * HEAD_DIM_B), 1.0)
    page_table = jax.random.permutation(ks[6], n_phys)[:DEC_BATCH * n_pages].reshape(
        DEC_BATCH, n_pages).astype(jnp.int32)

    w_in_a = nrm(ks[7], (N_A_LAYERS, D_MODEL, IN_PROJ_A), D_MODEL ** -0.5)
    conv_w_a = nrm(ks[8], (N_A_LAYERS, CONV_W, CONV_DIM), CONV_W ** -0.5)
    conv_b_a = nrm(ks[9], (N_A_LAYERS, CONV_DIM), 0.01)
    u = jax.random.uniform(ks[10], (N_A_LAYERS, N_HEADS_A), f32)
    dt0 = jnp.exp(u * (math.log(DT_MAX) - math.log(DT_MIN)) + math.log(DT_MIN))
    dt_bias_a = dt0 + jnp.log(-jnp.expm1(-dt0))
    a_log_a = jnp.log(jax.random.uniform(ks[11], (N_A_LAYERS, N_HEADS_A), f32, 1.0, 16.0))
    d_skip_a = 1.0 + nrm(ks[12], (N_A_LAYERS, N_HEADS_A), 0.1)
    norm_a = 1.0 + nrm(ks[13], (N_A_LAYERS, D_INNER), 0.02)
    w_out_a = nrm(ks[14], (N_A_LAYERS, D_INNER, D_MODEL), DN_BETA * D_INNER ** -0.5)

    w_kv = nrm(ks[15], (D_MODEL, QK_WIDTH_B + V_WIDTH_B), D_MODEL ** -0.5)
    w_q_b = nrm(ks[16], (N_B_LAYERS, D_MODEL, QK_WIDTH_B), D_MODEL ** -0.5)
    lambda_q1_b = nrm(ks[17], (N_B_LAYERS, HEAD_DIM_B), 0.1)
    lambda_k1_b = nrm(ks[18], (N_B_LAYERS, HEAD_DIM_B), 0.1)
    lambda_q2_b = nrm(ks[19], (N_B_LAYERS, HEAD_DIM_B), 0.1)
    lambda_k2_b = nrm(ks[20], (N_B_LAYERS, HEAD_DIM_B), 0.1)
    subln_b = 1.0 + nrm(ks[21], (N_B_LAYERS, 2 * HEAD_DIM_B), 0.02)
    w_o_b = nrm(ks[22], (N_B_LAYERS, V_WIDTH_B, D_MODEL), DN_BETA * V_WIDTH_B ** -0.5)

    w_up = nrm(ks[23], (DEPTH, D_MODEL, D_FF), D_MODEL ** -0.5)
    w_down = nrm(ks[24], (DEPTH, D_FF, D_MODEL), DN_BETA * D_FF ** -0.5)
    ln_g = 1.0 + nrm(ks[25], (DEPTH, 2, D_MODEL), 0.02)
    ln_b = nrm(ks[26], (DEPTH, 2, D_MODEL), 0.01)
    return {'x_prompt': x_prompt, 'x_sample': x_sample, 'state_conv': state_conv,
            'state_ssm': state_ssm, 'cache_k': cache_k, 'cache_v': cache_v,
            'page_table': page_table, 'w_in_a': w_in_a, 'conv_w_a': conv_w_a,
            'conv_b_a': conv_b_a, 'dt_bias_a': dt_bias_a, 'a_log_a': a_log_a,
            'd_skip_a': d_skip_a, 'norm_a': norm_a, 'w_out_a': w_out_a, 'w_kv': w_kv,
            'w_q_b': w_q_b, 'lambda_q1_b': lambda_q1_b, 'lambda_k1_b': lambda_k1_b,
            'lambda_q2_b': lambda_q2_b, 'lambda_k2_b': lambda_k2_b, 'subln_b': subln_b,
            'w_o_b': w_o_b, 'w_up': w_up, 'w_down': w_down, 'ln_g': ln_g, 'ln_b': ln_b}


def reference(x_prompt, x_sample, state_conv, state_ssm, cache_k, cache_v, page_table,
              w_in_a, conv_w_a, conv_b_a, dt_bias_a, a_log_a, d_skip_a, norm_a, w_out_a,
              w_kv, w_q_b, lambda_q1_b, lambda_k1_b, lambda_q2_b, lambda_k2_b, subln_b, w_o_b,
              w_up, w_down, ln_g, ln_b):
    params = dict(w_in_a=w_in_a, conv_w_a=conv_w_a, conv_b_a=conv_b_a, dt_bias_a=dt_bias_a,
                  a_log_a=a_log_a, d_skip_a=d_skip_a, norm_a=norm_a, w_out_a=w_out_a,
                  w_kv=w_kv, w_q_b=w_q_b, lambda_q1_b=lambda_q1_b, lambda_k1_b=lambda_k1_b,
                  lambda_q2_b=lambda_q2_b, lambda_k2_b=lambda_k2_b, subln_b=subln_b, w_o_b=w_o_b,
                  w_up=w_up, w_down=w_down, ln_g=ln_g, ln_b=ln_b)
    bp, sp = x_prompt.shape[:2]
    pos_p = jnp.arange(sp)
    conv0 = jnp.zeros((N_A_LAYERS, bp, CONV_W - 1, CONV_DIM), x_prompt.dtype)
    ssm0 = jnp.zeros((N_A_LAYERS, bp, N_HEADS_A, HEADDIM_A, D_STATE), x_prompt.dtype)
    y_prompt, conv_p, ssm_p, k_p, v_p = trunk(x_prompt, pos_p, conv0, ssm0, None, None, params)
    bd, ts = x_sample.shape[:2]
    n_pages = page_table.shape[1]
    pos_s = PAST_LEN + jnp.arange(ts)
    k_past = cache_k[page_table].reshape(bd, n_pages * PAGE_SIZE, N_HEADS_B, 2, HEAD_DIM_B)
    v_past = cache_v[page_table].reshape(bd, n_pages * PAGE_SIZE, N_HEADS_B, 2 * HEAD_DIM_B)
    y_sample, conv_s, ssm_s, k_s, v_s = trunk(x_sample, pos_s, state_conv, state_ssm,
                                              k_past, v_past, params)
    return (y_prompt, y_sample, conv_p, ssm_p, k_p, v_p, conv_s, ssm_s, k_s, v_s)
```

```python
import functools
import math

import jax
import jax.numpy as jnp
from jax import lax
from jax.experimental import pallas as pl
from jax.experimental.pallas import tpu as pltpu

F32 = jnp.float32
BF16 = jnp.bfloat16

D_MODEL = 1024
DEPTH = 2
PAST_LEN = 2048

D_INNER = 2048
HEADDIM_A = 64
N_HEADS_A = 32
D_STATE = 128
N_GROUPS_A = 8
HEADS_PER_GROUP = N_HEADS_A // N_GROUPS_A
GROUP_W = HEADS_PER_GROUP * HEADDIM_A
CONV_W = 4
CONV_DIM = D_INNER + 2 * N_GROUPS_A * D_STATE
SSD_CHUNK = 128
DT_REP = 4
ZX_W = D_INNER + CONV_DIM + DT_REP * N_HEADS_A

HEAD_DIM_B = 64
N_HEADS_B = 8
HEAD_W = 2 * HEAD_DIM_B
QK_W = N_HEADS_B * HEAD_W
ROPE_THETA = 10000.0

D_FF = 4 * D_MODEL
DN_ALPHA = (2 * DEPTH) ** 0.25
LN_EPS = 1e-5
RMS_EPS = 1e-5

LANES = 128
VMEM_LIMIT = 56 * 1024 * 1024


def _cparams(sem):
    return pltpu.CompilerParams(dimension_semantics=sem, vmem_limit_bytes=VMEM_LIMIT)


def _layer_norm(r, g, b):
    mu = jnp.mean(r, axis=-1, keepdims=True)
    d = r - mu
    var = jnp.mean(d * d, axis=-1, keepdims=True)
    return d * lax.rsqrt(var + LN_EPS) * g + b


def _silu(x):
    return x * (1.0 / (1.0 + jnp.exp(-x)))


def _softplus(x):
    return jnp.maximum(x, 0.0) + jnp.log1p(jnp.exp(-jnp.abs(x)))


def _split3(v):
    hi = v.astype(BF16).astype(F32)
    r1 = v - hi
    mid = r1.astype(BF16).astype(F32)
    lo = r1 - mid
    lane = lax.broadcasted_iota(jnp.int32, v.shape, 1)
    return jnp.where(lane < 32, hi, jnp.where(lane < 64, mid, jnp.where(lane < 96, lo, 0.0))).astype(BF16)


def _rep_matrix(width):
    r = jnp.arange(LANES)[:, None]
    c = jnp.arange(N_HEADS_A * width)[None, :]
    return ((r < 96) & ((r % 32) == (c // width))).astype(BF16)


def _proj_kernel(x_ref, w_ref, o_ref, xb_ref):
    @pl.when(pl.program_id(1) == 0)
    def _():
        xb_ref[...] = x_ref[...].astype(BF16)

    o_ref[...] = jnp.dot(xb_ref[...], w_ref[...], preferred_element_type=F32).astype(o_ref.dtype)


def _proj(x, w, tm, tn):
    m, k = x.shape
    n = w.shape[1]
    return pl.pallas_call(
        _proj_kernel,
        grid=(m // tm, n // tn),
        in_specs=[pl.BlockSpec((tm, k), lambda i, j: (i, 0)),
                  pl.BlockSpec((k, tn), lambda i, j: (0, j))],
        out_specs=pl.BlockSpec((tm, tn), lambda i, j: (i, j)),
        out_shape=jax.ShapeDtypeStruct((m, n), F32),
        scratch_shapes=[pltpu.VMEM((tm, k), BF16)],
        compiler_params=_cparams(("parallel", "arbitrary")),
        name="in_proj",
    )(x, w)


def _proj_ln_kernel(x_ref, w_ref, r_ref, g_ref, b_ref, o_ref):
    y = jnp.dot(x_ref[...], w_ref[...], preferred_element_type=F32)
    o_ref[...] = _layer_norm(DN_ALPHA * r_ref[...] + y, g_ref[...], b_ref[...])


def _proj_ln(x, w, resid, g, b, tm):
    m, k = x.shape
    n = w.shape[1]
    return pl.pallas_call(
        _proj_ln_kernel,
        grid=(m // tm,),
        in_specs=[pl.BlockSpec((tm, k), lambda i: (i, 0)),
                  pl.BlockSpec((k, n), lambda i: (0, 0)),
                  pl.BlockSpec((tm, n), lambda i: (i, 0)),
                  pl.BlockSpec((1, n), lambda i: (0, 0)),
                  pl.BlockSpec((1, n), lambda i: (0, 0))],
        out_specs=pl.BlockSpec((tm, n), lambda i: (i, 0)),
        out_shape=jax.ShapeDtypeStruct((m, n), F32),
        compiler_params=_cparams(("parallel",)),
        name="proj_ln",
    )(x, w, resid, g, b)


def _mlp_kernel(x_ref, wu_ref, wd_ref, g_ref, b_ref, o_ref, xb_ref, acc_ref):
    j = pl.program_id(1)

    @pl.when(j == 0)
    def _():
        xb_ref[...] = x_ref[...].astype(BF16)
        acc_ref[...] = jnp.zeros_like(acc_ref)

    h = jnp.dot(xb_ref[...], wu_ref[...], preferred_element_type=F32)
    h = jnp.square(jnp.maximum(h, 0.0)).astype(BF16)
    acc_ref[...] += jnp.dot(h, wd_ref[...], preferred_element_type=F32)

    @pl.when(j == pl.num_programs(1) - 1)
    def _():
        o_ref[...] = _layer_norm(DN_ALPHA * x_ref[...] + acc_ref[...], g_ref[...], b_ref[...])


def _mlp(x, w_up, w_down, g, b, tm, tf):
    m, d = x.shape
    ff = w_up.shape[1]
    return pl.pallas_call(
        _mlp_kernel,
        grid=(m // tm, ff // tf),
        in_specs=[pl.BlockSpec((tm, d), lambda i, j: (i, 0)),
                  pl.BlockSpec((d, tf), lambda i, j: (0, j)),
                  pl.BlockSpec((tf, d), lambda i, j: (j, 0)),
                  pl.BlockSpec((1, d), lambda i, j: (0, 0)),
                  pl.BlockSpec((1, d), lambda i, j: (0, 0))],
        out_specs=pl.BlockSpec((tm, d), lambda i, j: (i, 0)),
        out_shape=jax.ShapeDtypeStruct((m, d), F32),
        scratch_shapes=[pltpu.VMEM((tm, d), BF16), pltpu.VMEM((tm, d), F32)],
        compiler_params=_cparams(("parallel", "arbitrary")),
        name="mlp_ln",
    )(x, w_up, w_down, g, b)


def _shift_rows(cur, prev8, k):
    rolled = pltpu.roll(cur, k, 0)
    rolled_prev = pltpu.roll(prev8, k, 0)
    row = lax.broadcasted_iota(jnp.int32, prev8.shape, 0)
    top = jnp.where(row < k, rolled_prev, rolled[:8])
    return jnp.concatenate([top, rolled[8:]], axis=0)


def _ssd_kernel(z_ref, x_ref, bc_ref, dt_ref, cw_ref, cb_ref, dtb_ref, alog_ref, dskip_ref, ng_ref,
                r128_ref, r64_ref, y_ref, st_ref, halo_ref, state_ref):
    c = pl.program_id(1)
    q = SSD_CHUNK

    @pl.when(c == 0)
    def _():
        halo_ref[...] = jnp.zeros_like(halo_ref)
        state_ref[...] = jnp.zeros_like(state_ref)

    raw = jnp.concatenate([x_ref[0], bc_ref[0]], axis=1)
    prev8 = halo_ref[...]
    conv = cb_ref[...] + cw_ref[CONV_W - 1:CONV_W, :] * raw
    for k in range(1, CONV_W):
        conv = conv + cw_ref[CONV_W - 1 - k:CONV_W - k, :] * _shift_rows(raw, prev8, k)
    halo_ref[...] = raw[q - 8:, :]
    conv = _silu(conv)
    xc = conv[:, :D_INNER]
    bmat = conv[:, D_INNER:D_INNER + N_GROUPS_A * D_STATE]
    cmat = conv[:, D_INNER + N_GROUPS_A * D_STATE:]

    dt = _softplus(dt_ref[0] + dtb_ref[...])
    da = dt * (-jnp.exp(alog_ref[...]))
    ri = lax.broadcasted_iota(jnp.int32, (q, q), 0)
    ci = lax.broadcasted_iota(jnp.int32, (q, q), 1)
    causal = ri >= ci
    tri = causal.astype(BF16)
    hi = da.astype(BF16)
    r1 = da - hi.astype(F32)
    mid = r1.astype(BF16)
    lo = (r1 - mid.astype(F32)).astype(BF16)
    cs = (jnp.dot(tri, hi, preferred_element_type=F32)
          + jnp.dot(tri, mid, preferred_element_type=F32)
          + jnp.dot(tri, lo, preferred_element_type=F32))
    cs_t = cs.T
    cs_last = cs[q - 1:q, :]

    r64 = r64_ref[...]
    dt_rep = jnp.dot(_split3(dt), r64, preferred_element_type=F32)
    e_rep = jnp.dot(_split3(jnp.exp(cs)), r64, preferred_element_type=F32)
    dte_rep = jnp.dot(_split3(jnp.exp(cs_last - cs)), r64, preferred_element_type=F32)
    cs_rep = jnp.dot(_split3(cs), r128_ref[...], preferred_element_type=F32)

    xs = xc * dt_rep
    xs_b = xs.astype(BF16)
    xe_b = (xs * dte_rep).astype(BF16)
    e_last = e_rep[q - 1:q, :]

    y_groups = []
    for g in range(N_GROUPS_A):
        b_g = bmat[:, g * D_STATE:(g + 1) * D_STATE]
        c_g = cmat[:, g * D_STATE:(g + 1) * D_STATE].astype(BF16)
        b_gt = b_g.T.astype(BF16)
        cb = jnp.dot(c_g, b_gt, preferred_element_type=F32)
        gs = slice(g * GROUP_W, (g + 1) * GROUP_W)
        st_g = state_ref[g]
        y_off = jnp.dot(c_g, st_g.astype(BF16), preferred_element_type=F32) * e_rep[:, gs]
        y_heads = []
        for r in range(HEADS_PER_GROUP):
            h = g * HEADS_PER_GROUP + r
            diff = cs_rep[:, h * q:(h + 1) * q] - cs_t[h:h + 1, :]
            w = (cb * jnp.exp(jnp.where(causal, diff, -jnp.inf))).astype(BF16)
            y_heads.append(jnp.dot(w, xs_b[:, h * HEADDIM_A:(h + 1) * HEADDIM_A],
                                   preferred_element_type=F32))
        y_groups.append(jnp.concatenate(y_heads, axis=1) + y_off)
        state_ref[g] = st_g * e_last[:, gs] + jnp.dot(b_gt, xe_b[:, gs], preferred_element_type=F32)

    y = jnp.concatenate(y_groups, axis=1) + dskip_ref[...] * xc
    y = y * _silu(z_ref[0])
    outs = []
    for g in range(N_GROUPS_A):
        yg = y[:, g * GROUP_W:(g + 1) * GROUP_W]
        outs.append(yg * lax.rsqrt(jnp.mean(yg * yg, axis=-1, keepdims=True) + RMS_EPS))
    y_ref[0] = (jnp.concatenate(outs, axis=1) * ng_ref[...]).astype(y_ref.dtype)

    @pl.when(c == pl.num_programs(1) - 1)
    def _():
        for g in range(N_GROUPS_A):
            st_ref[0, g] = state_ref[g].T


def _ssd_prompt(zx, bsz, seq, cw, cb, dtb, alog, dskip, ng, r128, r64):
    q = SSD_CHUNK
    nc = seq // q
    dt_blk = (D_INNER + CONV_DIM) // LANES
    const = lambda shape: pl.BlockSpec(shape, lambda b, c: (0,) * len(shape))
    y, st = pl.pallas_call(
        _ssd_kernel,
        grid=(bsz, nc),
        in_specs=[pl.BlockSpec((1, q, D_INNER), lambda b, c: (b, c, 0)),
                  pl.BlockSpec((1, q, D_INNER), lambda b, c: (b, c, 1)),
                  pl.BlockSpec((1, q, D_INNER), lambda b, c: (b, c, 2)),
                  pl.BlockSpec((1, q, LANES), lambda b, c: (b, c, dt_blk)),
                  const((CONV_W, CONV_DIM)), const((1, CONV_DIM)), const((1, LANES)), const((1, LANES)),
                  const((1, D_INNER)), const((1, D_INNER)),
                  const((LANES, N_HEADS_A * q)), const((LANES, D_INNER))],
        out_specs=[pl.BlockSpec((1, q, D_INNER), lambda b, c: (b, c, 0)),
                   pl.BlockSpec((1, N_GROUPS_A, GROUP_W, D_STATE), lambda b, c: (b, 0, 0, 0))],
        out_shape=[jax.ShapeDtypeStruct((bsz, seq, D_INNER), BF16),
                   jax.ShapeDtypeStruct((bsz, N_GROUPS_A, GROUP_W, D_STATE), F32)],
        scratch_shapes=[pltpu.VMEM((8, CONV_DIM), F32),
                        pltpu.VMEM((N_GROUPS_A, D_STATE, GROUP_W), F32)],
        compiler_params=_cparams(("parallel", "arbitrary")),
        name="ssd_prompt",
    )(zx, zx, zx, zx, cw, cb, dtb, alog, dskip, ng, r128, r64)
    return y, st.reshape(bsz, N_HEADS_A, HEADDIM_A, D_STATE)


def _ssd_step_kernel(zx_ref, cst_ref, h_ref, cw_ref, cb_ref, dtb_ref, alog_ref, dskip_ref, ng_ref,
                     r64_ref, y_ref, cst_out_ref, h_out_ref):
    zx = zx_ref[0]
    cst = cst_ref[0]
    raw = zx[:, D_INNER:D_INNER + CONV_DIM]
    conv = cb_ref[...] + cw_ref[CONV_W - 1:CONV_W, :] * raw
    for k in range(CONV_W - 1):
        conv = conv + cw_ref[k:k + 1, :] * cst[:, k * CONV_DIM:(k + 1) * CONV_DIM]
    cst_out_ref[0] = jnp.concatenate([cst[:, CONV_DIM:], raw], axis=1)
    conv = _silu(conv)
    xc = conv[:, :D_INNER]
    bmat = conv[:, D_INNER:D_INNER + N_GROUPS_A * D_STATE]
    cmat = conv[:, D_INNER + N_GROUPS_A * D_STATE:]

    dt = _softplus(zx[:, D_INNER + CONV_DIM:] + dtb_ref[...])
    da = jnp.exp(dt * (-jnp.exp(alog_ref[...])))
    both = jnp.concatenate([jnp.broadcast_to(dt, (8, LANES)), jnp.broadcast_to(da, (8, LANES))], axis=0)
    rep = jnp.dot(_split3(both), r64_ref[...], preferred_element_type=F32)
    xdt = xc * rep[0:1, :]
    da_rep = rep[8:9, :]

    xdt_col = jnp.broadcast_to(xdt, (LANES, D_INNER)).T
    da_col = jnp.broadcast_to(da_rep, (LANES, D_INNER)).T

    row8 = lax.broadcasted_iota(jnp.int32, (8, D_STATE), 0)
    c_rows = jnp.zeros((8, D_STATE), F32)
    for g in range(N_GROUPS_A):
        c_rows = jnp.where(row8 == g, jnp.broadcast_to(cmat[:, g * D_STATE:(g + 1) * D_STATE], (8, D_STATE)), c_rows)
    c_rows = c_rows.astype(BF16)

    y_rows = []
    for g in range(N_GROUPS_A):
        gs = slice(g * GROUP_W, (g + 1) * GROUP_W)
        h_new = h_ref[0, gs, :] * da_col[gs, :] + xdt_col[gs, :] * bmat[:, g * D_STATE:(g + 1) * D_STATE]
        h_out_ref[0, gs, :] = h_new
        yg = lax.dot_general(c_rows, h_new.astype(BF16), (((1,), (1,)), ((), ())),
                             preferred_element_type=F32)
        y_rows.append(yg[g:g + 1, :])
    y = jnp.concatenate(y_rows, axis=1) + dskip_ref[...] * xc
    y = y * _silu(zx[:, :D_INNER])
    outs = []
    for g in range(N_GROUPS_A):
        yg = y[:, g * GROUP_W:(g + 1) * GROUP_W]
        outs.append(yg * lax.rsqrt(jnp.mean(yg * yg, axis=-1, keepdims=True) + RMS_EPS))
    y_ref[0] = (jnp.concatenate(outs, axis=1) * ng_ref[...]).astype(y_ref.dtype)


def _ssd_step(zx, conv_state, ssm_state, cw, cb, dtb, alog, dskip, ng, r64):
    n = zx.shape[0]
    const = lambda shape: pl.BlockSpec(shape, lambda b: (0,) * len(shape))
    row = lambda w: pl.BlockSpec((1, 1, w), lambda b: (b, 0, 0))
    cst_w = (CONV_W - 1) * CONV_DIM
    y, cst, h = pl.pallas_call(
        _ssd_step_kernel,
        grid=(n,),
        in_specs=[row(ZX_W), row(cst_w),
                  pl.BlockSpec((1, D_INNER, D_STATE), lambda b: (b, 0, 0)),
                  const((CONV_W, CONV_DIM)), const((1, CONV_DIM)), const((1, LANES)), const((1, LANES)),
                  const((1, D_INNER)), const((1, D_INNER)), const((LANES, D_INNER))],
        out_specs=[row(D_INNER), row(cst_w),
                   pl.BlockSpec((1, D_INNER, D_STATE), lambda b: (b, 0, 0))],
        out_shape=[jax.ShapeDtypeStruct((n, 1, D_INNER), BF16),
                   jax.ShapeDtypeStruct((n, 1, cst_w), F32),
                   jax.ShapeDtypeStruct((n, D_INNER, D_STATE), F32)],
        compiler_params=_cparams(("parallel",)),
        name="ssd_step",
    )(zx.reshape(n, 1, ZX_W), conv_state.reshape(n, 1, cst_w), ssm_state.reshape(n, D_INNER, D_STATE),
      cw, cb, dtb, alog, dskip, ng, r64)
    return (y.reshape(n, D_INNER), cst.reshape(n, CONV_W - 1, CONV_DIM),
            h.reshape(n, N_HEADS_A, HEADDIM_A, D_STATE))


def _rope_tables(pos_col):
    lane = lax.broadcasted_iota(jnp.int32, pos_col.shape, 1)
    half = HEAD_DIM_B // 2
    idx = (lane % half).astype(F32)
    inv = jnp.exp(idx * (-math.log(ROPE_THETA) / half))
    ang = pos_col * inv
    first = (lane % HEAD_DIM_B) < half
    return jnp.cos(ang), jnp.where(first, -jnp.sin(ang), jnp.sin(ang))


def _rope(x, cos, sin_signed):
    n = x.shape[1]
    reps = n // LANES
    cos_f = jnp.concatenate([cos] * reps, axis=1)
    sin_f = jnp.concatenate([sin_signed] * reps, axis=1)
    half = HEAD_DIM_B // 2
    first = (lax.broadcasted_iota(jnp.int32, x.shape, 1) % HEAD_DIM_B) < half
    rot = jnp.where(first, pltpu.roll(x, n - half, 1), pltpu.roll(x, half, 1))
    return x * cos_f + rot * sin_f


def _kvq_kernel(x_ref, w_ref, k_ref, v_ref, kb_ref, vb_ref, qb_ref, *, tm, seq, pos0):
    i = pl.program_id(0)
    xb = x_ref[...].astype(BF16)
    row = lax.broadcasted_iota(jnp.int32, (tm, LANES), 0) + i * tm
    pos = (row % seq + pos0).astype(F32)
    cos, sin_signed = _rope_tables(pos)

    k = _rope(jnp.dot(xb, w_ref[:, :QK_W], preferred_element_type=F32), cos, sin_signed)
    k_ref[...] = k
    kb_ref[...] = k.astype(BF16)
    v = jnp.dot(xb, w_ref[:, QK_W:2 * QK_W], preferred_element_type=F32)
    v_ref[...] = v
    vb_ref[...] = v.astype(BF16)
    qv = _rope(jnp.dot(xb, w_ref[:, 2 * QK_W:], preferred_element_type=F32), cos, sin_signed)
    qb_ref[...] = (qv * (HEAD_DIM_B ** -0.5)).astype(BF16)


def _kvq(x, w_kvq, tm, seq, pos0):
    m = x.shape[0]
    blk = pl.BlockSpec((tm, QK_W), lambda i: (i, 0))
    return pl.pallas_call(
        functools.partial(_kvq_kernel, tm=tm, seq=seq, pos0=pos0),
        grid=(m // tm,),
        in_specs=[pl.BlockSpec((tm, D_MODEL), lambda i: (i, 0)),
                  pl.BlockSpec((D_MODEL, 3 * QK_W), lambda i: (0, 0))],
        out_specs=[blk, blk, blk, blk, blk],
        out_shape=[jax.ShapeDtypeStruct((m, QK_W), F32), jax.ShapeDtypeStruct((m, QK_W), F32),
                   jax.ShapeDtypeStruct((m, QK_W), BF16), jax.ShapeDtypeStruct((m, QK_W), BF16),
                   jax.ShapeDtypeStruct((m, QK_W), BF16)],
        compiler_params=_cparams(("parallel",)),
        name="kvq_rope",
    )(x, w_kvq)


def _lambda(lq1_ref, lk1_ref, lq2_ref, lk2_ref, lam_init):
    s1 = jnp.sum(lq1_ref[...] * lk1_ref[...], axis=-1, keepdims=True)
    s2 = jnp.sum(lq2_ref[...] * lk2_ref[...], axis=-1, keepdims=True)
    return jnp.exp(s1) - jnp.exp(s2) + lam_init


def _head_rms(o, g, lam_init):
    return o * lax.rsqrt(jnp.mean(o * o, axis=-1, keepdims=True) + RMS_EPS) * g * (1.0 - lam_init)


def _attn_kernel(q_ref, k_ref, v_ref, lq1_ref, lk1_ref, lq2_ref, lk2_ref, g_ref, o_ref, *, t, lam_init):
    qi = pl.program_id(2)
    q = q_ref[0].astype(F32)
    lane = lax.broadcasted_iota(jnp.int32, (t, HEAD_W), 1)
    qs = jnp.concatenate([jnp.where(lane < HEAD_DIM_B, q, 0.0),
                          jnp.where(lane >= HEAD_DIM_B, q, 0.0)], axis=0).astype(BF16)

    def block(ki, carry, masked):
        m, l, acc = carry
        start = pl.multiple_of(ki * t, t)
        k = k_ref[0, pl.ds(start, t), :]
        v = v_ref[0, pl.ds(start, t), :]
        s = lax.dot_general(qs, k, (((1,), (1,)), ((), ())), preferred_element_type=F32)
        if masked:
            row = lax.broadcasted_iota(jnp.int32, (2 * t, t), 0) % t
            col = lax.broadcasted_iota(jnp.int32, (2 * t, t), 1)
            s = jnp.where(col <= row, s, -jnp.inf)
        m_new = jnp.maximum(m, jnp.max(s, axis=-1, keepdims=True))
        alpha = jnp.exp(m - m_new)
        p = jnp.exp(s - m_new)
        l = alpha * l + jnp.sum(p, axis=-1, keepdims=True)
        acc = alpha * acc + jnp.dot(p.astype(BF16), v, preferred_element_type=F32)
        return m_new, l, acc

    init = (jnp.full((2 * t, 1), -jnp.inf, F32), jnp.zeros((2 * t, 1), F32), jnp.zeros((2 * t, HEAD_W), F32))
    carry = lax.fori_loop(0, qi, lambda ki, cr: block(ki, cr, False), init)
    _, l, acc = block(qi, carry, True)
    lam = _lambda(lq1_ref, lk1_ref, lq2_ref, lk2_ref, lam_init)
    o = acc[:t] / l[:t] - lam * (acc[t:] / l[t:])
    o_ref[0] = _head_rms(o, g_ref[...], lam_init).astype(o_ref.dtype)


def _attn_prompt(qb, kb, vb, lq1, lk1, lq2, lk2, g, lam_init, t):
    bsz, seq, _ = qb.shape
    vec = pl.BlockSpec((1, HEAD_DIM_B), lambda b, h, i: (0, 0))
    return pl.pallas_call(
        functools.partial(_attn_kernel, t=t, lam_init=lam_init),
        grid=(bsz, N_HEADS_B, seq // t),
        in_specs=[pl.BlockSpec((1, t, HEAD_W), lambda b, h, i: (b, i, h)),
                  pl.BlockSpec((1, seq, HEAD_W), lambda b, h, i: (b, 0, h)),
                  pl.BlockSpec((1, seq, HEAD_W), lambda b, h, i: (b, 0, h)),
                  vec, vec, vec, vec,
                  pl.BlockSpec((1, HEAD_W), lambda b, h, i: (0, 0))],
        out_specs=pl.BlockSpec((1, t, HEAD_W), lambda b, h, i: (b, i, h)),
        out_shape=jax.ShapeDtypeStruct((bsz, seq, QK_W), BF16),
        compiler_params=_cparams(("parallel", "parallel", "arbitrary")),
        name="attn_prompt",
    )(qb, kb, vb, lq1, lk1, lq2, lk2, g)


def _attn_step_kernel(pt_ref, q_ref, kn_ref, vn_ref, lq1_ref, lk1_ref, lq2_ref, lk2_ref, g_ref, *rest,
                      pages, lam_init):
    k_refs = rest[:pages]
    v_refs = rest[pages:2 * pages]
    o_ref, qblk_ref, m_ref, l_ref, acc_ref = rest[2 * pages:]
    j = pl.program_id(1)
    nmap = 2 * N_HEADS_B

    @pl.when(j == 0)
    def _():
        rowi = lax.broadcasted_iota(jnp.int32, (nmap, QK_W), 0)
        coli = lax.broadcasted_iota(jnp.int32, (nmap, QK_W), 1)
        qrow = jnp.broadcast_to(q_ref[0].astype(F32), (nmap, QK_W))
        qblk_ref[...] = jnp.where(coli // HEAD_DIM_B == rowi, qrow, 0.0)
        m_ref[...] = jnp.full_like(m_ref, -jnp.inf)
        l_ref[...] = jnp.zeros_like(l_ref)
        acc_ref[...] = jnp.zeros_like(acc_ref)

    qblk = qblk_ref[...].astype(BF16)
    for i in range(pages):
        kp = k_refs[i][0].astype(BF16)
        vp = v_refs[i][0].astype(BF16)
        s = lax.dot_general(qblk, kp, (((1,), (1,)), ((), ())), preferred_element_type=F32)
        m_new = jnp.maximum(m_ref[...], jnp.max(s, axis=-1, keepdims=True))
        alpha = jnp.exp(m_ref[...] - m_new)
        p = jnp.exp(s - m_new)
        l_ref[...] = alpha * l_ref[...] + jnp.sum(p, axis=-1, keepdims=True)
        acc_ref[...] = alpha * acc_ref[...] + jnp.dot(p.astype(BF16), vp, preferred_element_type=F32)
        m_ref[...] = m_new

    @pl.when(j == pl.num_programs(1) - 1)
    def _():
        s_new = jnp.sum(qblk_ref[...] * kn_ref[0], axis=-1, keepdims=True)
        m_new = jnp.maximum(m_ref[...], s_new)
        alpha = jnp.exp(m_ref[...] - m_new)
        p_new = jnp.exp(s_new - m_new)
        l = alpha * l_ref[...] + p_new
        acc = alpha * acc_ref[...] + p_new * vn_ref[0]
        lam = _lambda(lq1_ref, lk1_ref, lq2_ref, lk2_ref, lam_init)
        rowi = lax.broadcasted_iota(jnp.int32, (nmap, QK_W), 0)
        coli = lax.broadcasted_iota(jnp.int32, (nmap, QK_W), 1)
        coef = jnp.where(rowi % 2 == 0, 1.0, -lam) / l
        o = jnp.sum(jnp.where(coli // HEAD_W == rowi // 2, acc * coef, 0.0), axis=0, keepdims=True)
        outs = [_head_rms(o[:, h * HEAD_W:(h + 1) * HEAD_W], g_ref[...], lam_init) for h in range(N_HEADS_B)]
        o_ref[0] = jnp.concatenate(outs, axis=1).astype(o_ref.dtype)


def _attn_step(q, k_new, v_new, cache_k, cache_v, page_table, lq1, lk1, lq2, lk2, g, lam_init, pages):
    n = q.shape[0]
    n_pages = page_table.shape[1]
    page = cache_k.shape[1]
    row = pl.BlockSpec((1, 1, QK_W), lambda b, j, pt: (b, 0, 0))
    vec = pl.BlockSpec((1, HEAD_DIM_B), lambda b, j, pt: (0, 0))

    def page_spec(i):
        return pl.BlockSpec((1, page, QK_W), lambda b, j, pt: (pt[b * n_pages + j * pages + i], 0, 0))

    grid_spec = pltpu.PrefetchScalarGridSpec(
        num_scalar_prefetch=1,
        grid=(n, n_pages // pages),
        in_specs=[row, row, row, vec, vec, vec, vec,
                  pl.BlockSpec((1, HEAD_W), lambda b, j, pt: (0, 0))]
                 + [page_spec(i) for i in range(pages)] * 2,
        out_specs=row,
        scratch_shapes=[pltpu.VMEM((2 * N_HEADS_B, QK_W), F32), pltpu.VMEM((2 * N_HEADS_B, 1), F32),
                        pltpu.VMEM((2 * N_HEADS_B, 1), F32), pltpu.VMEM((2 * N_HEADS_B, QK_W), F32)],
    )
    out = pl.pallas_call(
        functools.partial(_attn_step_kernel, pages=pages, lam_init=lam_init),
        grid_spec=grid_spec,
        out_shape=jax.ShapeDtypeStruct((n, 1, QK_W), BF16),
        compiler_params=_cparams(("parallel", "arbitrary")),
        name="attn_step",
    )(page_table.reshape(-1), q.reshape(n, 1, QK_W), k_new.reshape(n, 1, QK_W), v_new.reshape(n, 1, QK_W),
      lq1, lk1, lq2, lk2, g, *([cache_k] * pages), *([cache_v] * pages))
    return out.reshape(n, QK_W)


def _lambda_init(layer):
    return 0.8 - 0.6 * math.exp(-0.3 * layer)


def _row_tile(m, want):
    return want if m % want == 0 else m


def _trunk(x, bsz, seq, pos0, prm, mixer_a, mixer_b):
    m = x.shape[0]
    tm = _row_tile(m, 512)
    zx = _proj(x, prm["w_in"], _row_tile(m, 1024), ZX_W // 7)
    y, conv_state, ssm_state = mixer_a(zx)
    x = _proj_ln(y, prm["w_out"], x, prm["ln_g"][0, 0], prm["ln_b"][0, 0], tm)
    x = _mlp(x, prm["w_up"][0], prm["w_down"][0], prm["ln_g"][0, 1], prm["ln_b"][0, 1], tm, 512)
    k, v, kb, vb, qb = _kvq(x, prm["w_kvq"], _row_tile(m, 256), seq, pos0)
    o = mixer_b(qb, kb, vb, k, v)
    x = _proj_ln(o, prm["w_o"], x, prm["ln_g"][1, 0], prm["ln_b"][1, 0], tm)
    x = _mlp(x, prm["w_up"][1], prm["w_down"][1], prm["ln_g"][1, 1], prm["ln_b"][1, 1], tm, 512)
    return x, conv_state, ssm_state, k, v


def kernel(x_prompt, x_sample, state_conv, state_ssm, cache_k, cache_v, page_table, w_in_a, conv_w_a, conv_b_a, dt_bias_a, a_log_a, d_skip_a, norm_a, w_out_a, w_kv, w_q_b, lambda_q1_b, lambda_k1_b, lambda_q2_b, lambda_k2_b, subln_b, w_o_b, w_up, w_down, ln_g, ln_b):
    bp, sp, _ = x_prompt.shape
    bd, ts, _ = x_sample.shape
    assert ts == 1 and sp % 256 == 0

    w_in = w_in_a[0]
    split = D_INNER + CONV_DIM
    prm = {
        "w_in": jnp.concatenate([w_in[:, :split], jnp.tile(w_in[:, split:], (1, DT_REP))], axis=1).astype(BF16),
        "w_out": w_out_a[0].astype(BF16),
        "w_kvq": jnp.concatenate([w_kv, w_q_b[0]], axis=1).astype(BF16),
        "w_o": w_o_b[0].astype(BF16),
        "w_up": w_up.astype(BF16),
        "w_down": w_down.astype(BF16),
        "ln_g": ln_g.reshape(DEPTH, 2, 1, D_MODEL),
        "ln_b": ln_b.reshape(DEPTH, 2, 1, D_MODEL),
    }
    cw = conv_w_a[0]
    cb = conv_b_a[0].reshape(1, CONV_DIM)
    dtb = jnp.tile(dt_bias_a[0], DT_REP).reshape(1, LANES)
    alog = jnp.tile(a_log_a[0], DT_REP).reshape(1, LANES)
    dskip = jnp.repeat(d_skip_a[0], HEADDIM_A).reshape(1, D_INNER)
    ng = norm_a[0].reshape(1, D_INNER)
    r128 = _rep_matrix(SSD_CHUNK)
    r64 = _rep_matrix(HEADDIM_A)
    lam_init = _lambda_init(DEPTH // 2)
    lvec = [a[0].reshape(1, HEAD_DIM_B) for a in (lambda_q1_b, lambda_k1_b, lambda_q2_b, lambda_k2_b)]
    subln = subln_b[0].reshape(1, HEAD_W)

    def mixer_a_prompt(zx):
        y, st = _ssd_prompt(zx.reshape(bp, sp, ZX_W), bp, sp, cw, cb, dtb, alog, dskip, ng, r128, r64)
        conv_state = zx.reshape(bp, sp, ZX_W)[:, sp - (CONV_W - 1):, D_INNER:D_INNER + CONV_DIM]
        return y.reshape(bp * sp, D_INNER), conv_state, st

    def mixer_b_prompt(qb, kb, vb, k, v):
        shp = (bp, sp, QK_W)
        o = _attn_prompt(qb.reshape(shp), kb.reshape(shp), vb.reshape(shp), *lvec, subln, lam_init, 256)
        return o.reshape(bp * sp, QK_W)

    y_p, conv_p, ssm_p, k_p, v_p = _trunk(x_prompt.reshape(bp * sp, D_MODEL), bp, sp, 0, prm,
                                          mixer_a_prompt, mixer_b_prompt)

    n_phys, page = cache_k.shape[:2]

    def mixer_a_sample(zx):
        return _ssd_step(zx, state_conv[0], state_ssm[0], cw, cb, dtb, alog, dskip, ng, r64)

    def mixer_b_sample(qb, kb, vb, k, v):
        return _attn_step(qb, k, v, cache_k.reshape(n_phys, page, QK_W), cache_v.reshape(n_phys, page, QK_W),
                          page_table, *lvec, subln, lam_init, 8)

    y_s, conv_s, ssm_s, k_s, v_s = _trunk(x_sample.reshape(bd, D_MODEL), bd, 1, PAST_LEN, prm,
                                          mixer_a_sample, mixer_b_sample)

    return (y_p.reshape(bp, sp, D_MODEL), y_s.reshape(bd, 1, D_MODEL),
            conv_p[None], ssm_p[None],
            k_p.reshape(bp, sp, N_HEADS_B, 2, HEAD_DIM_B), v_p.reshape(bp, sp, N_HEADS_B, HEAD_W),
            conv_s[None], ssm_s[None],
            k_s.reshape(bd, 1, N_HEADS_B, 2, HEAD_DIM_B), v_s.reshape(bd, 1, N_HEADS_B, HEAD_W))
```

```python
import functools
import math

import jax
import jax.numpy as jnp
from jax import lax
from jax.experimental import pallas as pl
from jax.experimental.pallas import tpu as pltpu

F32 = jnp.float32
BF16 = jnp.bfloat16

D_MODEL = 1024
DEPTH = 2
PAST_LEN = 2048

D_INNER = 2048
HEADDIM_A = 64
N_HEADS_A = 32
D_STATE = 128
N_GROUPS_A = 8
HEADS_PER_GROUP = N_HEADS_A // N_GROUPS_A
GROUP_W = HEADS_PER_GROUP * HEADDIM_A
CONV_W = 4
CONV_DIM = D_INNER + 2 * N_GROUPS_A * D_STATE
SSD_CHUNK = 128
DT_REP = 4
ZX_W = D_INNER + CONV_DIM + DT_REP * N_HEADS_A

HEAD_DIM_B = 64
N_HEADS_B = 8
HEAD_W = 2 * HEAD_DIM_B
QK_W = N_HEADS_B * HEAD_W
ROPE_THETA = 10000.0
ATTN_TILE = 512
VT_ROWS = HEAD_W + 16
Q_SCALE_LOG2 = HEAD_DIM_B ** -0.5 * math.log2(math.e)

D_FF = 4 * D_MODEL
DN_ALPHA = (2 * DEPTH) ** 0.25
LN_EPS = 1e-5
RMS_EPS = 1e-5

LANES = 128
VMEM_LIMIT = 56 * 1024 * 1024


def _cparams(sem):
    return pltpu.CompilerParams(dimension_semantics=sem, vmem_limit_bytes=VMEM_LIMIT)


def _layer_norm(r, g, b):
    mu = jnp.mean(r, axis=-1, keepdims=True)
    d = r - mu
    var = jnp.mean(d * d, axis=-1, keepdims=True)
    return d * lax.rsqrt(var + LN_EPS) * g + b


def _silu(x):
    return x * (1.0 / (1.0 + jnp.exp(-x)))


def _softplus(x):
    return jnp.maximum(x, 0.0) + jnp.log1p(jnp.exp(-jnp.abs(x)))


def _split3(v):
    hi = v.astype(BF16).astype(F32)
    r1 = v - hi
    mid = r1.astype(BF16).astype(F32)
    lo = r1 - mid
    lane = lax.broadcasted_iota(jnp.int32, v.shape, 1)
    return jnp.where(lane < 32, hi, jnp.where(lane < 64, mid, jnp.where(lane < 96, lo, 0.0))).astype(BF16)


def _rep_matrix(width):
    r = jnp.arange(LANES)[:, None]
    c = jnp.arange(N_HEADS_A * width)[None, :]
    return ((r < 96) & ((r % 32) == (c // width))).astype(BF16)


def _proj_kernel(x_ref, w_ref, o_ref, xb_ref):
    @pl.when(pl.program_id(1) == 0)
    def _():
        xb_ref[...] = x_ref[...].astype(BF16)

    o_ref[...] = jnp.dot(xb_ref[...], w_ref[...], preferred_element_type=F32).astype(o_ref.dtype)


def _proj(x, w, tm, tn):
    m, k = x.shape
    n = w.shape[1]
    return pl.pallas_call(
        _proj_kernel,
        grid=(m // tm, n // tn),
        in_specs=[pl.BlockSpec((tm, k), lambda i, j: (i, 0)),
                  pl.BlockSpec((k, tn), lambda i, j: (0, j))],
        out_specs=pl.BlockSpec((tm, tn), lambda i, j: (i, j)),
        out_shape=jax.ShapeDtypeStruct((m, n), F32),
        scratch_shapes=[pltpu.VMEM((tm, k), BF16)],
        compiler_params=_cparams(("parallel", "arbitrary")),
        name="in_proj",
    )(x, w)


def _proj_ln_kernel(x_ref, w_ref, r_ref, g_ref, b_ref, o_ref):
    y = jnp.dot(x_ref[...].astype(BF16), w_ref[...], preferred_element_type=F32)
    o_ref[...] = _layer_norm(DN_ALPHA * r_ref[...] + y, g_ref[...], b_ref[...])


def _proj_ln(x, w, resid, g, b, tm):
    m, k = x.shape
    n = w.shape[1]
    return pl.pallas_call(
        _proj_ln_kernel,
        grid=(m // tm,),
        in_specs=[pl.BlockSpec((tm, k), lambda i: (i, 0)),
                  pl.BlockSpec((k, n), lambda i: (0, 0)),
                  pl.BlockSpec((tm, n), lambda i: (i, 0)),
                  pl.BlockSpec((1, n), lambda i: (0, 0)),
                  pl.BlockSpec((1, n), lambda i: (0, 0))],
        out_specs=pl.BlockSpec((tm, n), lambda i: (i, 0)),
        out_shape=jax.ShapeDtypeStruct((m, n), F32),
        compiler_params=_cparams(("parallel",)),
        name="proj_ln",
    )(x, w, resid, g, b)


def _mlp_kernel(x_ref, wu_ref, wd_ref, g_ref, b_ref, o_ref, xb_ref, acc_ref):
    j = pl.program_id(1)

    @pl.when(j == 0)
    def _():
        xb_ref[...] = x_ref[...].astype(BF16)
        acc_ref[...] = jnp.zeros_like(acc_ref)

    h = jnp.dot(xb_ref[...], wu_ref[...], preferred_element_type=F32)
    h = jnp.square(jnp.maximum(h, 0.0)).astype(BF16)
    acc_ref[...] += jnp.dot(h, wd_ref[...], preferred_element_type=F32)

    @pl.when(j == pl.num_programs(1) - 1)
    def _():
        o_ref[...] = _layer_norm(DN_ALPHA * x_ref[...] + acc_ref[...], g_ref[...], b_ref[...])


def _mlp(x, w_up, w_down, g, b, tm, tf):
    m, d = x.shape
    ff = w_up.shape[1]
    return pl.pallas_call(
        _mlp_kernel,
        grid=(m // tm, ff // tf),
        in_specs=[pl.BlockSpec((tm, d), lambda i, j: (i, 0)),
                  pl.BlockSpec((d, tf), lambda i, j: (0, j)),
                  pl.BlockSpec((tf, d), lambda i, j: (j, 0)),
                  pl.BlockSpec((1, d), lambda i, j: (0, 0)),
                  pl.BlockSpec((1, d), lambda i, j: (0, 0))],
        out_specs=pl.BlockSpec((tm, d), lambda i, j: (i, 0)),
        out_shape=jax.ShapeDtypeStruct((m, d), F32),
        scratch_shapes=[pltpu.VMEM((tm, d), BF16), pltpu.VMEM((tm, d), F32)],
        compiler_params=_cparams(("parallel", "arbitrary")),
        name="mlp_ln",
    )(x, w_up, w_down, g, b)


def _shift_rows(cur, prev8, k):
    rolled = pltpu.roll(cur, k, 0)
    rolled_prev = pltpu.roll(prev8, k, 0)
    row = lax.broadcasted_iota(jnp.int32, prev8.shape, 0)
    top = jnp.where(row < k, rolled_prev, rolled[:8])
    return jnp.concatenate([top, rolled[8:]], axis=0)


def _ssd_kernel(z_ref, x_ref, bc_ref, dt_ref, cw_ref, cb_ref, dtb_ref, alog_ref, dskip_ref, ng_ref,
                r128_ref, r64_ref, y_ref, st_ref, halo_ref, state_ref):
    c = pl.program_id(1)
    q = SSD_CHUNK

    @pl.when(c == 0)
    def _():
        halo_ref[...] = jnp.zeros_like(halo_ref)
        state_ref[...] = jnp.zeros_like(state_ref)

    raw = jnp.concatenate([x_ref[0], bc_ref[0]], axis=1)
    prev8 = halo_ref[...]
    conv = cb_ref[...] + cw_ref[CONV_W - 1:CONV_W, :] * raw
    for k in range(1, CONV_W):
        conv = conv + cw_ref[CONV_W - 1 - k:CONV_W - k, :] * _shift_rows(raw, prev8, k)
    halo_ref[...] = raw[q - 8:, :]
    conv = _silu(conv)
    xc = conv[:, :D_INNER]
    bmat = conv[:, D_INNER:D_INNER + N_GROUPS_A * D_STATE]
    cmat = conv[:, D_INNER + N_GROUPS_A * D_STATE:]

    dt = _softplus(dt_ref[0] + dtb_ref[...])
    da = dt * (-jnp.exp(alog_ref[...]))
    ri = lax.broadcasted_iota(jnp.int32, (q, q), 0)
    ci = lax.broadcasted_iota(jnp.int32, (q, q), 1)
    causal = ri >= ci
    tri = causal.astype(BF16)
    hi = da.astype(BF16)
    r1 = da - hi.astype(F32)
    mid = r1.astype(BF16)
    lo = (r1 - mid.astype(F32)).astype(BF16)
    cs = (jnp.dot(tri, hi, preferred_element_type=F32)
          + jnp.dot(tri, mid, preferred_element_type=F32)
          + jnp.dot(tri, lo, preferred_element_type=F32))
    cs_t = cs.T
    cs_last = cs[q - 1:q, :]

    r64 = r64_ref[...]
    dt_rep = jnp.dot(_split3(dt), r64, preferred_element_type=F32)
    e_rep = jnp.dot(_split3(jnp.exp(cs)), r64, preferred_element_type=F32)
    dte_rep = jnp.dot(_split3(jnp.exp(cs_last - cs)), r64, preferred_element_type=F32)
    cs_rep = jnp.dot(_split3(cs), r128_ref[...], preferred_element_type=F32)

    xs = xc * dt_rep
    xs_b = xs.astype(BF16)
    xe_b = (xs * dte_rep).astype(BF16)
    e_last = e_rep[q - 1:q, :]

    y_groups = []
    for g in range(N_GROUPS_A):
        b_g = bmat[:, g * D_STATE:(g + 1) * D_STATE]
        c_g = cmat[:, g * D_STATE:(g + 1) * D_STATE].astype(BF16)
        b_gt = b_g.T.astype(BF16)
        cb = jnp.dot(c_g, b_gt, preferred_element_type=F32)
        gs = slice(g * GROUP_W, (g + 1) * GROUP_W)
        st_g = state_ref[g]
        y_off = jnp.dot(c_g, st_g.astype(BF16), preferred_element_type=F32) * e_rep[:, gs]
        y_heads = []
        for r in range(HEADS_PER_GROUP):
            h = g * HEADS_PER_GROUP + r
            diff = cs_rep[:, h * q:(h + 1) * q] - cs_t[h:h + 1, :]
            w = (cb * jnp.exp(jnp.where(causal, diff, -jnp.inf))).astype(BF16)
            y_heads.append(jnp.dot(w, xs_b[:, h * HEADDIM_A:(h + 1) * HEADDIM_A],
                                   preferred_element_type=F32))
        y_groups.append(jnp.concatenate(y_heads, axis=1) + y_off)
        state_ref[g] = st_g * e_last[:, gs] + jnp.dot(b_gt, xe_b[:, gs], preferred_element_type=F32)

    y = jnp.concatenate(y_groups, axis=1) + dskip_ref[...] * xc
    y = y * _silu(z_ref[0])
    outs = []
    for g in range(N_GROUPS_A):
        yg = y[:, g * GROUP_W:(g + 1) * GROUP_W]
        outs.append(yg * lax.rsqrt(jnp.mean(yg * yg, axis=-1, keepdims=True) + RMS_EPS))
    y_ref[0] = (jnp.concatenate(outs, axis=1) * ng_ref[...]).astype(y_ref.dtype)

    @pl.when(c == pl.num_programs(1) - 1)
    def _():
        for g in range(N_GROUPS_A):
            st_ref[0, g] = state_ref[g].T


def _ssd_prompt(zx, bsz, seq, cw, cb, dtb, alog, dskip, ng, r128, r64):
    q = SSD_CHUNK
    nc = seq // q
    dt_blk = (D_INNER + CONV_DIM) // LANES
    const = lambda shape: pl.BlockSpec(shape, lambda b, c: (0,) * len(shape))
    y, st = pl.pallas_call(
        _ssd_kernel,
        grid=(bsz, nc),
        in_specs=[pl.BlockSpec((1, q, D_INNER), lambda b, c: (b, c, 0)),
                  pl.BlockSpec((1, q, D_INNER), lambda b, c: (b, c, 1)),
                  pl.BlockSpec((1, q, D_INNER), lambda b, c: (b, c, 2)),
                  pl.BlockSpec((1, q, LANES), lambda b, c: (b, c, dt_blk)),
                  const((CONV_W, CONV_DIM)), const((1, CONV_DIM)), const((1, LANES)), const((1, LANES)),
                  const((1, D_INNER)), const((1, D_INNER)),
                  const((LANES, N_HEADS_A * q)), const((LANES, D_INNER))],
        out_specs=[pl.BlockSpec((1, q, D_INNER), lambda b, c: (b, c, 0)),
                   pl.BlockSpec((1, N_GROUPS_A, GROUP_W, D_STATE), lambda b, c: (b, 0, 0, 0))],
        out_shape=[jax.ShapeDtypeStruct((bsz, seq, D_INNER), BF16),
                   jax.ShapeDtypeStruct((bsz, N_GROUPS_A, GROUP_W, D_STATE), F32)],
        scratch_shapes=[pltpu.VMEM((8, CONV_DIM), F32),
                        pltpu.VMEM((N_GROUPS_A, D_STATE, GROUP_W), F32)],
        compiler_params=_cparams(("parallel", "arbitrary")),
        name="ssd_prompt",
    )(zx, zx, zx, zx, cw, cb, dtb, alog, dskip, ng, r128, r64)
    return y, st.reshape(bsz, N_HEADS_A, HEADDIM_A, D_STATE)


def _ssd_step_kernel(zx_ref, cst_ref, h_ref, cw_ref, cb_ref, dtb_ref, alog_ref, dskip_ref, ng_ref,
                     r64_ref, y_ref, cst_out_ref, h_out_ref):
    r = pl.program_id(0) % 8
    zx = zx_ref[pl.ds(r, 1), :]
    raw = zx[:, D_INNER:D_INNER + CONV_DIM]
    conv = cb_ref[...] + cw_ref[CONV_W - 1:CONV_W, :] * raw
    for k in range(CONV_W - 1):
        prev = cst_ref[k, pl.ds(r, 1), :]
        conv = conv + cw_ref[k:k + 1, :] * prev
        if k > 0:
            cst_out_ref[k - 1, pl.ds(r, 1), :] = prev
    cst_out_ref[CONV_W - 2, pl.ds(r, 1), :] = raw
    conv = _silu(conv)
    xc = conv[:, :D_INNER]
    bmat = conv[:, D_INNER:D_INNER + N_GROUPS_A * D_STATE]
    cmat = conv[:, D_INNER + N_GROUPS_A * D_STATE:]

    dt = _softplus(zx[:, D_INNER + CONV_DIM:] + dtb_ref[...])
    da = jnp.exp(dt * (-jnp.exp(alog_ref[...])))
    both = jnp.concatenate([jnp.broadcast_to(dt, (8, LANES)), jnp.broadcast_to(da, (8, LANES))], axis=0)
    rep = jnp.dot(_split3(both), r64_ref[...], preferred_element_type=F32)
    xdt = xc * rep[0:1, :]
    da_rep = rep[8:9, :]

    xdt_col = jnp.broadcast_to(xdt, (LANES, D_INNER)).T
    da_col = jnp.broadcast_to(da_rep, (LANES, D_INNER)).T

    row8 = lax.broadcasted_iota(jnp.int32, (8, D_STATE), 0)
    c_rows = jnp.zeros((8, D_STATE), F32)
    for g in range(N_GROUPS_A):
        c_rows = jnp.where(row8 == g, jnp.broadcast_to(cmat[:, g * D_STATE:(g + 1) * D_STATE], (8, D_STATE)), c_rows)
    c_rows = c_rows.astype(BF16)

    y_rows = []
    for g in range(N_GROUPS_A):
        gs = slice(g * GROUP_W, (g + 1) * GROUP_W)
        h_new = h_ref[0, gs, :] * da_col[gs, :] + xdt_col[gs, :] * bmat[:, g * D_STATE:(g + 1) * D_STATE]
        h_out_ref[0, gs, :] = h_new
        yg = lax.dot_general(c_rows, h_new.astype(BF16), (((1,), (1,)), ((), ())),
                             preferred_element_type=F32)
        y_rows.append(yg[g:g + 1, :])
    y = jnp.concatenate(y_rows, axis=1) + dskip_ref[...] * xc
    y = y * _silu(zx[:, :D_INNER])
    outs = []
    for g in range(N_GROUPS_A):
        yg = y[:, g * GROUP_W:(g + 1) * GROUP_W]
        outs.append(yg * lax.rsqrt(jnp.mean(yg * yg, axis=-1, keepdims=True) + RMS_EPS))
    y_ref[pl.ds(r, 1), :] = jnp.concatenate(outs, axis=1) * ng_ref[...]


def _ssd_step(zx, conv_state_t, ssm_state, cw, cb, dtb, alog, dskip, ng, r64):
    n = zx.shape[0]
    const = lambda shape: pl.BlockSpec(shape, lambda b: (0,) * len(shape))
    rows = lambda w: pl.BlockSpec((8, w), lambda b: (b // 8, 0))
    cst_blk = pl.BlockSpec((CONV_W - 1, 8, CONV_DIM), lambda b: (0, b // 8, 0))
    y, cst, h = pl.pallas_call(
        _ssd_step_kernel,
        grid=(n,),
        in_specs=[rows(ZX_W), cst_blk,
                  pl.BlockSpec((1, D_INNER, D_STATE), lambda b: (b, 0, 0)),
                  const((CONV_W, CONV_DIM)), const((1, CONV_DIM)), const((1, LANES)), const((1, LANES)),
                  const((1, D_INNER)), const((1, D_INNER)), const((LANES, D_INNER))],
        out_specs=[rows(D_INNER), cst_blk,
                   pl.BlockSpec((1, D_INNER, D_STATE), lambda b: (b, 0, 0))],
        out_shape=[jax.ShapeDtypeStruct((n, D_INNER), F32),
                   jax.ShapeDtypeStruct((CONV_W - 1, n, CONV_DIM), F32),
                   jax.ShapeDtypeStruct((n, D_INNER, D_STATE), F32)],
        compiler_params=_cparams(("arbitrary",)),
        name="ssd_step",
    )(zx, conv_state_t, ssm_state.reshape(n, D_INNER, D_STATE), cw, cb, dtb, alog, dskip, ng, r64)
    return y, cst, h.reshape(n, N_HEADS_A, HEADDIM_A, D_STATE)


def _rope_tables(pos_col):
    lane = lax.broadcasted_iota(jnp.int32, pos_col.shape, 1)
    half = HEAD_DIM_B // 2
    idx = (lane % half).astype(F32)
    inv = jnp.exp(idx * (-math.log(ROPE_THETA) / half))
    ang = pos_col * inv
    first = (lane % HEAD_DIM_B) < half
    return jnp.cos(ang), jnp.where(first, -jnp.sin(ang), jnp.sin(ang))


def _rope(x, cos, sin_signed):
    n = x.shape[1]
    reps = n // LANES
    cos_f = jnp.concatenate([cos] * reps, axis=1)
    sin_f = jnp.concatenate([sin_signed] * reps, axis=1)
    half = HEAD_DIM_B // 2
    first = (lax.broadcasted_iota(jnp.int32, x.shape, 1) % HEAD_DIM_B) < half
    rot = jnp.where(first, pltpu.roll(x, n - half, 1), pltpu.roll(x, half, 1))
    return x * cos_f + rot * sin_f


def _kvq_values(x_ref, w_ref, tm, seq, pos0):
    i = pl.program_id(0)
    xb = x_ref[...].astype(BF16)
    row = lax.broadcasted_iota(jnp.int32, (tm, LANES), 0) + i * tm
    pos = (row % seq + pos0).astype(F32)
    cos, sin_signed = _rope_tables(pos)
    k = _rope(jnp.dot(xb, w_ref[:, :QK_W], preferred_element_type=F32), cos, sin_signed)
    v = jnp.dot(xb, w_ref[:, QK_W:2 * QK_W], preferred_element_type=F32)
    q = _rope(jnp.dot(xb, w_ref[:, 2 * QK_W:], preferred_element_type=F32), cos, sin_signed)
    return k, v, q


def _kvq_prompt_kernel(x_ref, w_ref, kt_ref, v_ref, kb_ref, vt_ref, qb_ref, *, tm, seq):
    k, v, q = _kvq_values(x_ref, w_ref, tm, seq, 0)
    kt_ref[0] = k.T
    kb_ref[...] = k.astype(BF16)
    v_ref[...] = v
    vt_ref[0, :, 0, :HEAD_W, :] = v.T.reshape(N_HEADS_B, HEAD_W, tm).astype(BF16)
    vt_ref[0, :, 0, HEAD_W:, :] = jnp.ones((N_HEADS_B, VT_ROWS - HEAD_W, tm), BF16)
    qb_ref[...] = (q * Q_SCALE_LOG2).astype(BF16)


def _kvq_prompt(x, w_kvq, bsz, seq, tm):
    m = x.shape[0]
    nblk = seq // tm
    blk = pl.BlockSpec((tm, QK_W), lambda i: (i, 0))
    return pl.pallas_call(
        functools.partial(_kvq_prompt_kernel, tm=tm, seq=seq),
        grid=(m // tm,),
        in_specs=[pl.BlockSpec((tm, D_MODEL), lambda i: (i, 0)),
                  pl.BlockSpec((D_MODEL, 3 * QK_W), lambda i: (0, 0))],
        out_specs=[pl.BlockSpec((1, QK_W, tm), lambda i: (i // nblk, 0, i % nblk)),
                   blk, blk,
                   pl.BlockSpec((1, N_HEADS_B, 1, VT_ROWS, tm), lambda i: (i // nblk, 0, i % nblk, 0, 0)),
                   blk],
        out_shape=[jax.ShapeDtypeStruct((bsz, QK_W, seq), F32), jax.ShapeDtypeStruct((m, QK_W), F32),
                   jax.ShapeDtypeStruct((m, QK_W), BF16),
                   jax.ShapeDtypeStruct((bsz, N_HEADS_B, nblk, VT_ROWS, tm), BF16),
                   jax.ShapeDtypeStruct((m, QK_W), BF16)],
        compiler_params=_cparams(("parallel",)),
        name="kvq_prompt",
    )(x, w_kvq)


def _kvq_step_kernel(x_ref, w_ref, k_ref, v_ref, q_ref, *, tm, pos0):
    k, v, q = _kvq_values(x_ref, w_ref, tm, 1, pos0)
    k_ref[...] = k
    v_ref[...] = v
    q_ref[...] = q * (HEAD_DIM_B ** -0.5)


def _kvq_step(x, w_kvq, pos0):
    m = x.shape[0]
    blk = pl.BlockSpec((m, QK_W), lambda i: (0, 0))
    return pl.pallas_call(
        functools.partial(_kvq_step_kernel, tm=m, pos0=pos0),
        grid=(1,),
        in_specs=[pl.BlockSpec((m, D_MODEL), lambda i: (0, 0)),
                  pl.BlockSpec((D_MODEL, 3 * QK_W), lambda i: (0, 0))],
        out_specs=[blk, blk, blk],
        out_shape=[jax.ShapeDtypeStruct((m, QK_W), F32)] * 3,
        compiler_params=_cparams(("arbitrary",)),
        name="kvq_step",
    )(x, w_kvq)


def _lambda(lq1_ref, lk1_ref, lq2_ref, lk2_ref, lam_init):
    s1 = jnp.sum(lq1_ref[...] * lk1_ref[...], axis=-1, keepdims=True)
    s2 = jnp.sum(lq2_ref[...] * lk2_ref[...], axis=-1, keepdims=True)
    return jnp.exp(s1) - jnp.exp(s2) + lam_init


def _head_rms(o, g, lam_init):
    return o * lax.rsqrt(jnp.mean(o * o, axis=-1, keepdims=True) + RMS_EPS) * g * (1.0 - lam_init)


def _attn_kernel(q_ref, k_ref, vt_ref, lq1_ref, lk1_ref, lq2_ref, lk2_ref, g_ref, o_ref,
                 qs_ref, s0_ref, s1_ref, m_ref, acc_ref, *, t, lam_init):
    qi = pl.program_id(2)
    q = q_ref[0].astype(F32)
    lane = lax.broadcasted_iota(jnp.int32, (t, HEAD_W), 1)
    qs_ref[...] = jnp.concatenate([jnp.where(lane < HEAD_DIM_B, q, 0.0),
                                   jnp.where(lane >= HEAD_DIM_B, q, 0.0)], axis=0).astype(BF16)
    m_ref[...] = jnp.full_like(m_ref, -jnp.inf)
    acc_ref[...] = jnp.zeros_like(acc_ref)

    def scores(ki, s_ref):
        start = pl.multiple_of(ki * t, t)
        k = k_ref[0, pl.ds(start, t), :]
        s_ref[...] = lax.dot_general(k, qs_ref[...], (((1,), (1,)), ((), ())),
                                     preferred_element_type=F32)

    def update(ki, s_ref, masked):
        s = s_ref[...]
        if masked:
            key = lax.broadcasted_iota(jnp.int32, (t, 2 * t), 0)
            qry = lax.broadcasted_iota(jnp.int32, (t, 2 * t), 1) % t
            s = jnp.where(key <= qry, s, -jnp.inf)
        m_old = m_ref[...]
        m_new = jnp.maximum(m_old, jnp.max(s, axis=0, keepdims=True))
        p = jnp.exp2(s - m_new).astype(BF16)
        vt = vt_ref[0, 0, ki]
        acc_ref[...] = jnp.exp2(m_old - m_new) * acc_ref[...] + jnp.dot(vt, p, preferred_element_type=F32)
        m_ref[...] = m_new

    scores(0, s0_ref)

    def pair(p, carry):
        scores(2 * p + 1, s1_ref)
        update(2 * p, s0_ref, False)
        scores(2 * p + 2, s0_ref)
        update(2 * p + 1, s1_ref, False)
        return carry

    lax.fori_loop(0, qi // 2, pair, 0)

    @pl.when(qi % 2 == 0)
    def _():
        update(qi, s0_ref, True)

    @pl.when(qi % 2 == 1)
    def _():
        scores(qi, s1_ref)
        update(qi - 1, s0_ref, False)
        update(qi, s1_ref, True)

    lam = _lambda(lq1_ref, lk1_ref, lq2_ref, lk2_ref, lam_init)
    acc = acc_ref[:HEAD_W, :]
    l = acc_ref[HEAD_W:HEAD_W + 1, :]
    o = acc[:, :t] / l[:, :t] - lam * (acc[:, t:] / l[:, t:])
    o = o * lax.rsqrt(jnp.mean(o * o, axis=0, keepdims=True) + RMS_EPS) * g_ref[...] * (1.0 - lam_init)
    o_ref[0] = o.T.astype(o_ref.dtype)


def _attn_prompt(qb, kb, vtb, lq1, lk1, lq2, lk2, g_col, lam_init, t):
    bsz, seq, _ = qb.shape
    vec = pl.BlockSpec((1, HEAD_DIM_B), lambda b, h, i: (0, 0))
    return pl.pallas_call(
        functools.partial(_attn_kernel, t=t, lam_init=lam_init),
        grid=(bsz, N_HEADS_B, seq // t),
        in_specs=[pl.BlockSpec((1, t, HEAD_W), lambda b, h, i: (b, i, h)),
                  pl.BlockSpec((1, seq, HEAD_W), lambda b, h, i: (b, 0, h)),
                  pl.BlockSpec((1, 1, seq // t, VT_ROWS, t), lambda b, h, i: (b, h, 0, 0, 0)),
                  vec, vec, vec, vec,
                  pl.BlockSpec((HEAD_W, 1), lambda b, h, i: (0, 0))],
        out_specs=pl.BlockSpec((1, t, HEAD_W), lambda b, h, i: (b, i, h)),
        out_shape=jax.ShapeDtypeStruct((bsz, seq, QK_W), BF16),
        scratch_shapes=[pltpu.VMEM((2 * t, HEAD_W), BF16), pltpu.VMEM((t, 2 * t), F32), pltpu.VMEM((t, 2 * t), F32),
                        pltpu.VMEM((1, 2 * t), F32), pltpu.VMEM((VT_ROWS, 2 * t), F32)],
        compiler_params=_cparams(("parallel", "parallel", "arbitrary")),
        name="attn_prompt",
    )(qb, kb, vtb, lq1, lk1, lq2, lk2, g_col)


def _attn_step_kernel(pt_ref, q_ref, kn_ref, vn_ref, lq1_ref, lk1_ref, lq2_ref, lk2_ref, g_ref, rep_ref, *rest,
                      pages, page, lam_init):
    k_refs = rest[:pages]
    v_refs = rest[pages:2 * pages]
    o_ref, qblk_ref, m_ref, l_ref, acc_ref = rest[2 * pages:]
    r = pl.program_id(0) % 8
    j = pl.program_id(1)
    nmap = 2 * N_HEADS_B

    @pl.when(j == 0)
    def _():
        rowi = lax.broadcasted_iota(jnp.int32, (nmap, QK_W), 0)
        coli = lax.broadcasted_iota(jnp.int32, (nmap, QK_W), 1)
        qrow = jnp.broadcast_to(q_ref[pl.ds(r, 1), :], (nmap, QK_W))
        qblk_ref[...] = jnp.where(coli // HEAD_DIM_B == rowi, qrow, 0.0)
        m_ref[...] = jnp.full_like(m_ref, -jnp.inf)
        l_ref[...] = jnp.zeros_like(l_ref)
        acc_ref[...] = jnp.zeros_like(acc_ref)

    qblk = qblk_ref[...].astype(BF16)
    rowi = lax.broadcasted_iota(jnp.int32, (nmap, page * N_HEADS_B), 0)
    coli = lax.broadcasted_iota(jnp.int32, (nmap, page * N_HEADS_B), 1)
    own_head = coli % N_HEADS_B == rowi // 2
    s = jnp.concatenate([jnp.dot(qblk, k_refs[i][0].astype(BF16), preferred_element_type=F32)
                         for i in range(pages)], axis=1)
    m_new = jnp.maximum(m_ref[...], jnp.max(s, axis=-1, keepdims=True))
    alpha = jnp.exp(m_ref[...] - m_new)
    p = jnp.exp(s - m_new)
    l_ref[...] = alpha * l_ref[...] + jnp.sum(p, axis=-1, keepdims=True)
    m_ref[...] = m_new
    p = p.astype(BF16)
    pv = jnp.zeros((nmap, HEAD_W), F32)
    for i in range(pages):
        p_rep = jnp.dot(p[:, i * page:(i + 1) * page], rep_ref[...], preferred_element_type=F32)
        p_sel = jnp.where(own_head, p_rep, 0.0).astype(BF16)
        pv = pv + jnp.dot(p_sel, v_refs[i][0].astype(BF16), preferred_element_type=F32)
    acc_ref[...] = alpha * acc_ref[...] + pv

    @pl.when(j == pl.num_programs(1) - 1)
    def _():
        s_new = jnp.sum(qblk_ref[...] * kn_ref[pl.ds(r, 1), :], axis=-1, keepdims=True)
        m_new = jnp.maximum(m_ref[...], s_new)
        alpha = jnp.exp(m_ref[...] - m_new)
        p_new = jnp.exp(s_new - m_new)
        l = alpha * l_ref[...] + p_new
        vn = vn_ref[pl.ds(r, 1), :]
        row16 = lax.broadcasted_iota(jnp.int32, (nmap, HEAD_W), 0)
        vn_rows = jnp.zeros((nmap, HEAD_W), F32)
        for h in range(N_HEADS_B):
            vn_rows = jnp.where(row16 // 2 == h, jnp.broadcast_to(vn[:, h * HEAD_W:(h + 1) * HEAD_W], (nmap, HEAD_W)),
                                vn_rows)
        acc = alpha * acc_ref[...] + p_new * vn_rows
        lam = _lambda(lq1_ref, lk1_ref, lq2_ref, lk2_ref, lam_init)
        scaled = acc * (jnp.where(row16 % 2 == 0, 1.0, -lam) / l)
        outs = [_head_rms(scaled[2 * h:2 * h + 1, :] + scaled[2 * h + 1:2 * h + 2, :], g_ref[...], lam_init)
                for h in range(N_HEADS_B)]
        o_ref[pl.ds(r, 1), :] = jnp.concatenate(outs, axis=1)


def _attn_step(q, k_new, v_new, cache_kt, cache_vf, page_table, lq1, lk1, lq2, lk2, g, lam_init, pages):
    n = q.shape[0]
    n_pages = page_table.shape[1]
    page = cache_kt.shape[2]
    rows = pl.BlockSpec((8, QK_W), lambda b, j, pt: (b // 8, 0))
    vec = pl.BlockSpec((1, HEAD_DIM_B), lambda b, j, pt: (0, 0))
    rep = (jnp.arange(page)[:, None] == jnp.arange(page * N_HEADS_B)[None, :] // N_HEADS_B).astype(BF16)

    def page_spec(shape, i):
        return pl.BlockSpec((1,) + shape, lambda b, j, pt: (pt[b * n_pages + j * pages + i], 0, 0))

    grid_spec = pltpu.PrefetchScalarGridSpec(
        num_scalar_prefetch=1,
        grid=(n, n_pages // pages),
        in_specs=[rows, rows, rows, vec, vec, vec, vec,
                  pl.BlockSpec((1, HEAD_W), lambda b, j, pt: (0, 0)),
                  pl.BlockSpec((page, page * N_HEADS_B), lambda b, j, pt: (0, 0))]
                 + [page_spec((QK_W, page), i) for i in range(pages)]
                 + [page_spec((page * N_HEADS_B, HEAD_W), i) for i in range(pages)],
        out_specs=rows,
        scratch_shapes=[pltpu.VMEM((2 * N_HEADS_B, QK_W), F32), pltpu.VMEM((2 * N_HEADS_B, 1), F32),
                        pltpu.VMEM((2 * N_HEADS_B, 1), F32), pltpu.VMEM((2 * N_HEADS_B, HEAD_W), F32)],
    )
    return pl.pallas_call(
        functools.partial(_attn_step_kernel, pages=pages, page=page, lam_init=lam_init),
        grid_spec=grid_spec,
        out_shape=jax.ShapeDtypeStruct((n, QK_W), F32),
        compiler_params=_cparams(("arbitrary", "arbitrary")),
        name="attn_step",
    )(page_table.reshape(-1), q, k_new, v_new, lq1, lk1, lq2, lk2, g, rep,
      *([cache_kt] * pages), *([cache_vf] * pages))


def _lambda_init(layer):
    return 0.8 - 0.6 * math.exp(-0.3 * layer)


def _row_tile(m, want):
    return want if m % want == 0 else m


def _trunk(x, prm, mixer_a, mixer_b):
    m = x.shape[0]
    tm = _row_tile(m, 512)
    zx = _proj(x, prm["w_in"], _row_tile(m, 1024), ZX_W // 7)
    y, conv_state, ssm_state = mixer_a(zx)
    x = _proj_ln(y, prm["w_out"], x, prm["ln_g"][0, 0], prm["ln_b"][0, 0], tm)
    x = _mlp(x, prm["w_up"][0], prm["w_down"][0], prm["ln_g"][0, 1], prm["ln_b"][0, 1], tm, 512)
    o, k, v = mixer_b(x)
    x = _proj_ln(o, prm["w_o"], x, prm["ln_g"][1, 0], prm["ln_b"][1, 0], tm)
    x = _mlp(x, prm["w_up"][1], prm["w_down"][1], prm["ln_g"][1, 1], prm["ln_b"][1, 1], tm, 512)
    return x, conv_state, ssm_state, k, v


def kernel(x_prompt, x_sample, state_conv, state_ssm, cache_k, cache_v, page_table, w_in_a, conv_w_a, conv_b_a, dt_bias_a, a_log_a, d_skip_a, norm_a, w_out_a, w_kv, w_q_b, lambda_q1_b, lambda_k1_b, lambda_q2_b, lambda_k2_b, subln_b, w_o_b, w_up, w_down, ln_g, ln_b):
    bp, sp, _ = x_prompt.shape
    bd, ts, _ = x_sample.shape
    assert ts == 1 and sp % ATTN_TILE == 0 and bd % 8 == 0

    w_in = w_in_a[0]
    split = D_INNER + CONV_DIM
    prm = {
        "w_in": jnp.concatenate([w_in[:, :split], jnp.tile(w_in[:, split:], (1, DT_REP))], axis=1).astype(BF16),
        "w_out": w_out_a[0].astype(BF16),
        "w_kvq": jnp.concatenate([w_kv, w_q_b[0]], axis=1).astype(BF16),
        "w_o": w_o_b[0].astype(BF16),
        "w_up": w_up.astype(BF16),
        "w_down": w_down.astype(BF16),
        "ln_g": ln_g.reshape(DEPTH, 2, 1, D_MODEL),
        "ln_b": ln_b.reshape(DEPTH, 2, 1, D_MODEL),
    }
    cw = conv_w_a[0]
    cb = conv_b_a[0].reshape(1, CONV_DIM)
    dtb = jnp.tile(dt_bias_a[0], DT_REP).reshape(1, LANES)
    alog = jnp.tile(a_log_a[0], DT_REP).reshape(1, LANES)
    dskip = jnp.repeat(d_skip_a[0], HEADDIM_A).reshape(1, D_INNER)
    ng = norm_a[0].reshape(1, D_INNER)
    r128 = _rep_matrix(SSD_CHUNK)
    r64 = _rep_matrix(HEADDIM_A)
    lam_init = _lambda_init(DEPTH // 2)
    lvec = [a[0].reshape(1, HEAD_DIM_B) for a in (lambda_q1_b, lambda_k1_b, lambda_q2_b, lambda_k2_b)]
    subln = subln_b[0]

    def mixer_a_prompt(zx):
        y, st = _ssd_prompt(zx.reshape(bp, sp, ZX_W), bp, sp, cw, cb, dtb, alog, dskip, ng, r128, r64)
        conv_state = zx.reshape(bp, sp, ZX_W)[:, sp - (CONV_W - 1):, D_INNER:D_INNER + CONV_DIM]
        return y.reshape(bp * sp, D_INNER), conv_state, st

    def mixer_b_prompt(x):
        kt, v, kb, vtb, qb = _kvq_prompt(x, prm["w_kvq"], bp, sp, ATTN_TILE)
        shp = (bp, sp, QK_W)
        o = _attn_prompt(qb.reshape(shp), kb.reshape(shp), vtb, *lvec, subln.reshape(HEAD_W, 1), lam_init, ATTN_TILE)
        k = kt.reshape(bp, N_HEADS_B, 2, HEAD_DIM_B, sp).transpose(0, 4, 1, 2, 3)
        return o.reshape(bp * sp, QK_W), k, v.reshape(bp, sp, N_HEADS_B, HEAD_W)

    y_p, conv_p, ssm_p, k_p, v_p = _trunk(x_prompt.reshape(bp * sp, D_MODEL), prm, mixer_a_prompt, mixer_b_prompt)

    n_phys, page = cache_k.shape[:2]
    cache_kt = cache_k.transpose(0, 2, 3, 4, 1).reshape(n_phys, QK_W, page)
    cache_vf = cache_v.reshape(n_phys, page * N_HEADS_B, HEAD_W)

    def mixer_a_sample(zx):
        y, cst, h = _ssd_step(zx, state_conv[0].transpose(1, 0, 2), state_ssm[0], cw, cb, dtb, alog, dskip, ng, r64)
        return y, cst.transpose(1, 0, 2), h

    def mixer_b_sample(x):
        k, v, q = _kvq_step(x, prm["w_kvq"], PAST_LEN)
        o = _attn_step(q, k, v, cache_kt, cache_vf, page_table, *lvec, subln.reshape(1, HEAD_W), lam_init, 8)
        return o, k.reshape(bd, 1, N_HEADS_B, 2, HEAD_DIM_B), v.reshape(bd, 1, N_HEADS_B, HEAD_W)

    y_s, conv_s, ssm_s, k_s, v_s = _trunk(x_sample.reshape(bd, D_MODEL), prm, mixer_a_sample, mixer_b_sample)

    return (y_p.reshape(bp, sp, D_MODEL), y_s.reshape(bd, 1, D_MODEL),
            conv_p[None], ssm_p[None], k_p, v_p, conv_s[None], ssm_s[None], k_s, v_s)
```

```python
import functools
import math

import jax
import jax.numpy as jnp
from jax import lax
from jax.experimental import pallas as pl
from jax.experimental.pallas import tpu as pltpu

F32 = jnp.float32
BF16 = jnp.bfloat16

D_MODEL = 1024
DEPTH = 2
PAST_LEN = 2048

D_INNER = 2048
HEADDIM_A = 64
N_HEADS_A = 32
D_STATE = 128
N_GROUPS_A = 8
HEADS_PER_GROUP = N_HEADS_A // N_GROUPS_A
GROUP_W = HEADS_PER_GROUP * HEADDIM_A
CONV_W = 4
CONV_DIM = D_INNER + 2 * N_GROUPS_A * D_STATE
SSD_CHUNK = 128
DT_REP = 4
ZX_W = D_INNER + CONV_DIM

HEAD_DIM_B = 64
N_HEADS_B = 8
HEAD_W = 2 * HEAD_DIM_B
QK_W = N_HEADS_B * HEAD_W
ROPE_THETA = 10000.0
ATTN_TILE = 512
STEP_SEQS = 1
STEP_PAGES = 8
VT_ROWS = HEAD_W + 16
Q_SCALE_LOG2 = HEAD_DIM_B ** -0.5 * math.log2(math.e)

D_FF = 4 * D_MODEL
DN_ALPHA = (2 * DEPTH) ** 0.25
LN_EPS = 1e-5
RMS_EPS = 1e-5

LANES = 128
VMEM_LIMIT = 56 * 1024 * 1024


def _cparams(sem):
    return pltpu.CompilerParams(dimension_semantics=sem, vmem_limit_bytes=VMEM_LIMIT)


def _layer_norm(r, g, b):
    mu = jnp.mean(r, axis=-1, keepdims=True)
    d = r - mu
    var = jnp.mean(d * d, axis=-1, keepdims=True)
    return d * lax.rsqrt(var + LN_EPS) * g + b


def _silu(x):
    h = 0.5 * x
    return h + h * jnp.tanh(h)


def _softplus(x):
    return jnp.maximum(x, 0.0) + jnp.log1p(jnp.exp(-jnp.abs(x)))


def _split3(v):
    hi = v.astype(BF16).astype(F32)
    r1 = v - hi
    mid = r1.astype(BF16).astype(F32)
    lo = r1 - mid
    lane = lax.broadcasted_iota(jnp.int32, v.shape, 1)
    return jnp.where(lane < 32, hi, jnp.where(lane < 64, mid, jnp.where(lane < 96, lo, 0.0))).astype(BF16)


def _rep_matrix(width):
    r = jnp.arange(LANES)[:, None]
    c = jnp.arange(N_HEADS_A * width)[None, :]
    return ((r < 96) & ((r % 32) == (c // width))).astype(BF16)


def _proj_kernel(x_ref, w_ref, o_ref, xb_ref):
    @pl.when(pl.program_id(1) == 0)
    def _():
        xb_ref[...] = x_ref[...].astype(BF16)

    o_ref[...] = jnp.dot(xb_ref[...], w_ref[...], preferred_element_type=F32).astype(o_ref.dtype)


def _proj(x, w, tm, tn):
    m, k = x.shape
    n = w.shape[1]
    return pl.pallas_call(
        _proj_kernel,
        grid=(m // tm, n // tn),
        in_specs=[pl.BlockSpec((tm, k), lambda i, j: (i, 0)),
                  pl.BlockSpec((k, tn), lambda i, j: (0, j))],
        out_specs=pl.BlockSpec((tm, tn), lambda i, j: (i, j)),
        out_shape=jax.ShapeDtypeStruct((m, n), F32),
        scratch_shapes=[pltpu.VMEM((tm, k), BF16)],
        compiler_params=_cparams(("parallel", "arbitrary")),
        name="in_proj",
    )(x, w)


def _proj_ln_kernel(x_ref, w_ref, r_ref, g_ref, b_ref, o_ref):
    y = jnp.dot(x_ref[...].astype(BF16), w_ref[...], preferred_element_type=F32)
    o_ref[...] = _layer_norm(DN_ALPHA * r_ref[...] + y, g_ref[...], b_ref[...])


def _proj_ln(x, w, resid, g, b, tm):
    m, k = x.shape
    n = w.shape[1]
    return pl.pallas_call(
        _proj_ln_kernel,
        grid=(m // tm,),
        in_specs=[pl.BlockSpec((tm, k), lambda i: (i, 0)),
                  pl.BlockSpec((k, n), lambda i: (0, 0)),
                  pl.BlockSpec((tm, n), lambda i: (i, 0)),
                  pl.BlockSpec((1, n), lambda i: (0, 0)),
                  pl.BlockSpec((1, n), lambda i: (0, 0))],
        out_specs=pl.BlockSpec((tm, n), lambda i: (i, 0)),
        out_shape=jax.ShapeDtypeStruct((m, n), F32),
        compiler_params=_cparams(("parallel",)),
        name="proj_ln",
    )(x, w, resid, g, b)


def _mlp_kernel(x_ref, wu_ref, wd_ref, g_ref, b_ref, o_ref, xb_ref, acc_ref):
    j = pl.program_id(1)

    @pl.when(j == 0)
    def _():
        xb_ref[...] = x_ref[...].astype(BF16)
        acc_ref[...] = jnp.zeros_like(acc_ref)

    h = jnp.dot(xb_ref[...], wu_ref[...], preferred_element_type=F32)
    h = jnp.square(jnp.maximum(h, 0.0)).astype(BF16)
    acc_ref[...] += jnp.dot(h, wd_ref[...], preferred_element_type=F32)

    @pl.when(j == pl.num_programs(1) - 1)
    def _():
        o_ref[...] = _layer_norm(DN_ALPHA * x_ref[...] + acc_ref[...], g_ref[...], b_ref[...])


def _mlp(x, w_up, w_down, g, b, tm, tf):
    m, d = x.shape
    ff = w_up.shape[1]
    return pl.pallas_call(
        _mlp_kernel,
        grid=(m // tm, ff // tf),
        in_specs=[pl.BlockSpec((tm, d), lambda i, j: (i, 0)),
                  pl.BlockSpec((d, tf), lambda i, j: (0, j)),
                  pl.BlockSpec((tf, d), lambda i, j: (j, 0)),
                  pl.BlockSpec((1, d), lambda i, j: (0, 0)),
                  pl.BlockSpec((1, d), lambda i, j: (0, 0))],
        out_specs=pl.BlockSpec((tm, d), lambda i, j: (i, 0)),
        out_shape=jax.ShapeDtypeStruct((m, d), F32),
        scratch_shapes=[pltpu.VMEM((tm, d), BF16), pltpu.VMEM((tm, d), F32)],
        compiler_params=_cparams(("parallel", "arbitrary")),
        name="mlp_ln",
    )(x, w_up, w_down, g, b)


def _shift_rows(cur, prev8, k):
    rolled = pltpu.roll(cur, k, 0)
    rolled_prev = pltpu.roll(prev8, k, 0)
    row = lax.broadcasted_iota(jnp.int32, prev8.shape, 0)
    top = jnp.where(row < k, rolled_prev, rolled[:8])
    return jnp.concatenate([top, rolled[8:]], axis=0)


def _ssd_kernel(z_ref, x_ref, bc_ref, dt_ref, cw_ref, cb_ref, dtb_ref, alog_ref, dskip_ref, ng_ref,
                r128_ref, r64_ref, y_ref, st_ref, halo_ref, state_ref):
    c = pl.program_id(1)
    q = SSD_CHUNK

    @pl.when(c == 0)
    def _():
        halo_ref[...] = jnp.zeros_like(halo_ref)
        state_ref[...] = jnp.zeros_like(state_ref)

    raw = jnp.concatenate([x_ref[0], bc_ref[0]], axis=1)
    prev8 = halo_ref[...]
    conv = cb_ref[...] + cw_ref[CONV_W - 1:CONV_W, :] * raw
    for k in range(1, CONV_W):
        conv = conv + cw_ref[CONV_W - 1 - k:CONV_W - k, :] * _shift_rows(raw, prev8, k)
    halo_ref[...] = raw[q - 8:, :]
    conv = _silu(conv)
    xc = conv[:, :D_INNER]
    bmat = conv[:, D_INNER:D_INNER + N_GROUPS_A * D_STATE]
    cmat = conv[:, D_INNER + N_GROUPS_A * D_STATE:]

    dt = _softplus(dt_ref[0] + dtb_ref[...])
    da = dt * (-jnp.exp(alog_ref[...]))
    ri = lax.broadcasted_iota(jnp.int32, (q, q), 0)
    ci = lax.broadcasted_iota(jnp.int32, (q, q), 1)
    causal = ri >= ci
    tri = causal.astype(BF16)
    hi = da.astype(BF16)
    r1 = da - hi.astype(F32)
    mid = r1.astype(BF16)
    lo = (r1 - mid.astype(F32)).astype(BF16)
    cs = (jnp.dot(tri, hi, preferred_element_type=F32)
          + jnp.dot(tri, mid, preferred_element_type=F32)
          + jnp.dot(tri, lo, preferred_element_type=F32))
    cs_t = cs.T
    cs_last = cs[q - 1:q, :]

    r64 = r64_ref[...]
    dt_rep = jnp.dot(_split3(dt), r64, preferred_element_type=F32)
    e_rep = jnp.dot(_split3(jnp.exp(cs)), r64, preferred_element_type=F32)
    dte_rep = jnp.dot(_split3(jnp.exp(cs_last - cs)), r64, preferred_element_type=F32)
    cs_rep = jnp.dot(_split3(cs), r128_ref[...], preferred_element_type=F32)

    xs = xc * dt_rep
    xs_b = xs.astype(BF16)
    xe_b = (xs * dte_rep).astype(BF16)
    e_last = e_rep[q - 1:q, :]

    y_groups = []
    for g in range(N_GROUPS_A):
        b_g = bmat[:, g * D_STATE:(g + 1) * D_STATE]
        c_g = cmat[:, g * D_STATE:(g + 1) * D_STATE].astype(BF16)
        b_gt = b_g.T.astype(BF16)
        cb = jnp.dot(c_g, b_gt, preferred_element_type=F32)
        gs = slice(g * GROUP_W, (g + 1) * GROUP_W)
        st_g = state_ref[g]
        y_off = jnp.dot(c_g, st_g.astype(BF16), preferred_element_type=F32) * e_rep[:, gs]
        y_heads = []
        for r in range(HEADS_PER_GROUP):
            h = g * HEADS_PER_GROUP + r
            diff = cs_rep[:, h * q:(h + 1) * q] - cs_t[h:h + 1, :]
            w = (cb * jnp.exp(jnp.where(causal, diff, -jnp.inf))).astype(BF16)
            y_heads.append(jnp.dot(w, xs_b[:, h * HEADDIM_A:(h + 1) * HEADDIM_A],
                                   preferred_element_type=F32))
        y_groups.append(jnp.concatenate(y_heads, axis=1) + y_off)
        state_ref[g] = st_g * e_last[:, gs] + jnp.dot(b_gt, xe_b[:, gs], preferred_element_type=F32)

    y = jnp.concatenate(y_groups, axis=1) + dskip_ref[...] * xc
    y = y * _silu(z_ref[0])
    outs = []
    for g in range(N_GROUPS_A):
        yg = y[:, g * GROUP_W:(g + 1) * GROUP_W]
        outs.append(yg * lax.rsqrt(jnp.mean(yg * yg, axis=-1, keepdims=True) + RMS_EPS))
    y_ref[0] = (jnp.concatenate(outs, axis=1) * ng_ref[...]).astype(y_ref.dtype)

    @pl.when(c == pl.num_programs(1) - 1)
    def _():
        for g in range(N_GROUPS_A):
            st_ref[0, g] = state_ref[g].T


def _ssd_prompt(zx, dtr, bsz, seq, cw, cb, dtb, alog, dskip, ng, r128, r64):
    q = SSD_CHUNK
    nc = seq // q
    const = lambda shape: pl.BlockSpec(shape, lambda b, c: (0,) * len(shape))
    y, st = pl.pallas_call(
        _ssd_kernel,
        grid=(bsz, nc),
        in_specs=[pl.BlockSpec((1, q, D_INNER), lambda b, c: (b, c, 0)),
                  pl.BlockSpec((1, q, D_INNER), lambda b, c: (b, c, 1)),
                  pl.BlockSpec((1, q, D_INNER), lambda b, c: (b, c, 2)),
                  pl.BlockSpec((1, q, LANES), lambda b, c: (b, c, 0)),
                  const((CONV_W, CONV_DIM)), const((1, CONV_DIM)), const((1, LANES)), const((1, LANES)),
                  const((1, D_INNER)), const((1, D_INNER)),
                  const((LANES, N_HEADS_A * q)), const((LANES, D_INNER))],
        out_specs=[pl.BlockSpec((1, q, D_INNER), lambda b, c: (b, c, 0)),
                   pl.BlockSpec((1, N_GROUPS_A, GROUP_W, D_STATE), lambda b, c: (b, 0, 0, 0))],
        out_shape=[jax.ShapeDtypeStruct((bsz, seq, D_INNER), BF16),
                   jax.ShapeDtypeStruct((bsz, N_GROUPS_A, GROUP_W, D_STATE), F32)],
        scratch_shapes=[pltpu.VMEM((8, CONV_DIM), F32),
                        pltpu.VMEM((N_GROUPS_A, D_STATE, GROUP_W), F32)],
        compiler_params=_cparams(("parallel", "arbitrary")),
        name="ssd_prompt",
    )(zx, zx, zx, dtr, cw, cb, dtb, alog, dskip, ng, r128, r64)
    return y, st.reshape(bsz, N_HEADS_A, HEADDIM_A, D_STATE)


def _ssd_step_pre_kernel(dtr_ref, dtb_ref, alog_ref, r64_ref, r128_ref, dtrep_ref, dabc_ref):
    dt = _softplus(dtr_ref[...] + dtb_ref[...])
    da = jnp.exp(dt * (-jnp.exp(alog_ref[...])))
    dtrep_ref[...] = jnp.dot(_split3(dt), r64_ref[...], preferred_element_type=F32)
    dabc_ref[...] = jnp.dot(_split3(da), r128_ref[...], preferred_element_type=F32)


def _ssd_step_pre(dtr, dtb, alog, r64, r128):
    n = dtr.shape[0]
    full = lambda shape: pl.BlockSpec(shape, lambda i: (0,) * len(shape))
    return pl.pallas_call(
        _ssd_step_pre_kernel,
        grid=(1,),
        in_specs=[full((n, LANES)), full((1, LANES)), full((1, LANES)),
                  full((LANES, D_INNER)), full((LANES, N_HEADS_A * LANES))],
        out_specs=[full((n, D_INNER)), full((n, N_HEADS_A * LANES))],
        out_shape=[jax.ShapeDtypeStruct((n, D_INNER), F32), jax.ShapeDtypeStruct((n, N_HEADS_A * LANES), F32)],
        compiler_params=_cparams(("arbitrary",)),
        name="ssd_step_pre",
    )(dtr, dtb, alog, r64, r128)


def _ssd_step_kernel(zx_ref, dtrep_ref, dabc_ref, cst_ref, h_ref, cw_ref, cb_ref, dskip_ref, ng_ref,
                     y_ref, cst_out_ref, h_out_ref):
    r = pl.program_id(0) % 8
    zx = zx_ref[pl.ds(r, 1), :]
    raw = zx[:, D_INNER:D_INNER + CONV_DIM]
    conv = cb_ref[...] + cw_ref[CONV_W - 1:CONV_W, :] * raw
    for k in range(CONV_W - 1):
        prev = cst_ref[k, pl.ds(r, 1), :]
        conv = conv + cw_ref[k:k + 1, :] * prev
        if k > 0:
            cst_out_ref[k - 1, pl.ds(r, 1), :] = prev
    cst_out_ref[CONV_W - 2, pl.ds(r, 1), :] = raw
    conv = _silu(conv)
    xc = conv[:, :D_INNER]
    bmat = conv[:, D_INNER:D_INNER + N_GROUPS_A * D_STATE]
    cmat = conv[:, D_INNER + N_GROUPS_A * D_STATE:]

    xdt = xc * dtrep_ref[pl.ds(r, 1), :]
    da = dabc_ref[pl.ds(r, 1), :]

    xdt_col = jnp.broadcast_to(xdt, (LANES, D_INNER)).T

    row8 = lax.broadcasted_iota(jnp.int32, (8, D_STATE), 0)
    c_rows = jnp.zeros((8, D_STATE), F32)
    for g in range(N_GROUPS_A):
        c_rows = jnp.where(row8 == g, jnp.broadcast_to(cmat[:, g * D_STATE:(g + 1) * D_STATE], (8, D_STATE)), c_rows)
    c_rows = c_rows.astype(BF16)

    y_rows = []
    for g in range(N_GROUPS_A):
        gs = slice(g * GROUP_W, (g + 1) * GROUP_W)
        b_g = bmat[:, g * D_STATE:(g + 1) * D_STATE]
        heads = []
        for hh in range(g * HEADS_PER_GROUP, (g + 1) * HEADS_PER_GROUP):
            hs = slice(hh * HEADDIM_A, (hh + 1) * HEADDIM_A)
            heads.append(h_ref[0, hs, :] * da[:, hh * LANES:(hh + 1) * LANES] + xdt_col[hs, :] * b_g)
        h_new = jnp.concatenate(heads, axis=0)
        h_out_ref[0, gs, :] = h_new
        yg = lax.dot_general(c_rows, h_new.astype(BF16), (((1,), (1,)), ((), ())),
                             preferred_element_type=F32)
        y_rows.append(yg[g:g + 1, :])
    y = jnp.concatenate(y_rows, axis=1) + dskip_ref[...] * xc
    y = y * _silu(zx[:, :D_INNER])
    outs = []
    for g in range(N_GROUPS_A):
        yg = y[:, g * GROUP_W:(g + 1) * GROUP_W]
        outs.append(yg * lax.rsqrt(jnp.mean(yg * yg, axis=-1, keepdims=True) + RMS_EPS))
    y_ref[pl.ds(r, 1), :] = jnp.concatenate(outs, axis=1) * ng_ref[...]


def _ssd_step(zx, dtr, conv_state_t, ssm_state, cw, cb, dtb, alog, dskip, ng, r64, r128):
    n = zx.shape[0]
    dt_rep, da_bc = _ssd_step_pre(dtr, dtb, alog, r64, r128)
    const = lambda shape: pl.BlockSpec(shape, lambda b: (0,) * len(shape))
    rows = lambda w: pl.BlockSpec((8, w), lambda b: (b // 8, 0))
    cst_blk = pl.BlockSpec((CONV_W - 1, 8, CONV_DIM), lambda b: (0, b // 8, 0))
    y, cst, h = pl.pallas_call(
        _ssd_step_kernel,
        grid=(n,),
        in_specs=[rows(ZX_W), rows(D_INNER), rows(N_HEADS_A * LANES), cst_blk,
                  pl.BlockSpec((1, D_INNER, D_STATE), lambda b: (b, 0, 0)),
                  const((CONV_W, CONV_DIM)), const((1, CONV_DIM)), const((1, D_INNER)), const((1, D_INNER))],
        out_specs=[rows(D_INNER), cst_blk,
                   pl.BlockSpec((1, D_INNER, D_STATE), lambda b: (b, 0, 0))],
        out_shape=[jax.ShapeDtypeStruct((n, D_INNER), F32),
                   jax.ShapeDtypeStruct((CONV_W - 1, n, CONV_DIM), F32),
                   jax.ShapeDtypeStruct((n, D_INNER, D_STATE), F32)],
        compiler_params=_cparams(("arbitrary",)),
        name="ssd_step",
    )(zx, dt_rep, da_bc, conv_state_t, ssm_state.reshape(n, D_INNER, D_STATE), cw, cb, dskip, ng)
    return y, cst, h.reshape(n, N_HEADS_A, HEADDIM_A, D_STATE)


def _rope_tables(pos_col):
    lane = lax.broadcasted_iota(jnp.int32, pos_col.shape, 1)
    half = HEAD_DIM_B // 2
    idx = (lane % half).astype(F32)
    inv = jnp.exp(idx * (-math.log(ROPE_THETA) / half))
    ang = pos_col * inv
    first = (lane % HEAD_DIM_B) < half
    return jnp.cos(ang), jnp.where(first, -jnp.sin(ang), jnp.sin(ang))


def _rope(x, cos, sin_signed):
    n = x.shape[1]
    reps = n // LANES
    cos_f = jnp.concatenate([cos] * reps, axis=1)
    sin_f = jnp.concatenate([sin_signed] * reps, axis=1)
    half = HEAD_DIM_B // 2
    first = (lax.broadcasted_iota(jnp.int32, x.shape, 1) % HEAD_DIM_B) < half
    rot = jnp.where(first, pltpu.roll(x, n - half, 1), pltpu.roll(x, half, 1))
    return x * cos_f + rot * sin_f


def _kvq_values(x_ref, w_ref, tm, seq, pos0):
    i = pl.program_id(0)
    xb = x_ref[...].astype(BF16)
    row = lax.broadcasted_iota(jnp.int32, (tm, LANES), 0) + i * tm
    pos = (row % seq + pos0).astype(F32)
    cos, sin_signed = _rope_tables(pos)
    k = _rope(jnp.dot(xb, w_ref[:, :QK_W], preferred_element_type=F32), cos, sin_signed)
    v = jnp.dot(xb, w_ref[:, QK_W:2 * QK_W], preferred_element_type=F32)
    q = _rope(jnp.dot(xb, w_ref[:, 2 * QK_W:], preferred_element_type=F32), cos, sin_signed)
    return k, v, q


def _kvq_prompt_kernel(x_ref, w_ref, kt_ref, v_ref, kb_ref, vt_ref, qb_ref, *, tm, seq):
    k, v, q = _kvq_values(x_ref, w_ref, tm, seq, 0)
    kt_ref[0] = k.T
    kb_ref[...] = k.astype(BF16)
    v_ref[...] = v
    vt_ref[0, :, 0, :HEAD_W, :] = v.T.reshape(N_HEADS_B, HEAD_W, tm).astype(BF16)
    vt_ref[0, :, 0, HEAD_W:, :] = jnp.ones((N_HEADS_B, VT_ROWS - HEAD_W, tm), BF16)
    qb_ref[...] = (q * Q_SCALE_LOG2).astype(BF16)


def _kvq_prompt(x, w_kvq, bsz, seq, tm):
    m = x.shape[0]
    nblk = seq // tm
    blk = pl.BlockSpec((tm, QK_W), lambda i: (i, 0))
    return pl.pallas_call(
        functools.partial(_kvq_prompt_kernel, tm=tm, seq=seq),
        grid=(m // tm,),
        in_specs=[pl.BlockSpec((tm, D_MODEL), lambda i: (i, 0)),
                  pl.BlockSpec((D_MODEL, 3 * QK_W), lambda i: (0, 0))],
        out_specs=[pl.BlockSpec((1, QK_W, tm), lambda i: (i // nblk, 0, i % nblk)),
                   blk, blk,
                   pl.BlockSpec((1, N_HEADS_B, 1, VT_ROWS, tm), lambda i: (i // nblk, 0, i % nblk, 0, 0)),
                   blk],
        out_shape=[jax.ShapeDtypeStruct((bsz, QK_W, seq), F32), jax.ShapeDtypeStruct((m, QK_W), F32),
                   jax.ShapeDtypeStruct((m, QK_W), BF16),
                   jax.ShapeDtypeStruct((bsz, N_HEADS_B, nblk, VT_ROWS, tm), BF16),
                   jax.ShapeDtypeStruct((m, QK_W), BF16)],
        compiler_params=_cparams(("parallel",)),
        name="kvq_prompt",
    )(x, w_kvq)


def _kvq_step_kernel(x_ref, w_ref, k_ref, v_ref, q_ref, *, tm, pos0):
    k, v, q = _kvq_values(x_ref, w_ref, tm, 1, pos0)
    k_ref[...] = k
    v_ref[...] = v
    q_ref[...] = q * (HEAD_DIM_B ** -0.5)


def _kvq_step(x, w_kvq, pos0):
    m = x.shape[0]
    blk = pl.BlockSpec((m, QK_W), lambda i: (0, 0))
    return pl.pallas_call(
        functools.partial(_kvq_step_kernel, tm=m, pos0=pos0),
        grid=(1,),
        in_specs=[pl.BlockSpec((m, D_MODEL), lambda i: (0, 0)),
                  pl.BlockSpec((D_MODEL, 3 * QK_W), lambda i: (0, 0))],
        out_specs=[blk, blk, blk],
        out_shape=[jax.ShapeDtypeStruct((m, QK_W), F32)] * 3,
        compiler_params=_cparams(("arbitrary",)),
        name="kvq_step",
    )(x, w_kvq)


def _lambda(lq1_ref, lk1_ref, lq2_ref, lk2_ref, lam_init):
    s1 = jnp.sum(lq1_ref[...] * lk1_ref[...], axis=-1, keepdims=True)
    s2 = jnp.sum(lq2_ref[...] * lk2_ref[...], axis=-1, keepdims=True)
    return jnp.exp(s1) - jnp.exp(s2) + lam_init


def _head_rms(o, g, lam_init):
    return o * lax.rsqrt(jnp.mean(o * o, axis=-1, keepdims=True) + RMS_EPS) * g * (1.0 - lam_init)


def _attn_kernel(q_ref, k_ref, vt_ref, lq1_ref, lk1_ref, lq2_ref, lk2_ref, g_ref, o_ref,
                 qs_ref, s0_ref, s1_ref, m_ref, acc_ref, *, t, lam_init):
    qi = pl.program_id(2)
    q = q_ref[0].astype(F32)
    lane = lax.broadcasted_iota(jnp.int32, (t, HEAD_W), 1)
    qs_ref[...] = jnp.concatenate([jnp.where(lane < HEAD_DIM_B, q, 0.0),
                                   jnp.where(lane >= HEAD_DIM_B, q, 0.0)], axis=0).astype(BF16)
    m_ref[...] = jnp.full_like(m_ref, -jnp.inf)
    acc_ref[...] = jnp.zeros_like(acc_ref)

    def scores(ki, s_ref):
        start = pl.multiple_of(ki * t, t)
        k = k_ref[0, pl.ds(start, t), :]
        s_ref[...] = lax.dot_general(k, qs_ref[...], (((1,), (1,)), ((), ())),
                                     preferred_element_type=F32)

    def update(ki, s_ref, masked):
        s = s_ref[...]
        if masked:
            key = lax.broadcasted_iota(jnp.int32, (t, 2 * t), 0)
            qry = lax.broadcasted_iota(jnp.int32, (t, 2 * t), 1) % t
            s = jnp.where(key <= qry, s, -jnp.inf)
        m_old = m_ref[...]
        m_new = jnp.maximum(m_old, jnp.max(s, axis=0, keepdims=True))
        p = jnp.exp2(s - m_new).astype(BF16)
        vt = vt_ref[0, 0, ki]
        acc_ref[...] = jnp.exp2(m_old - m_new) * acc_ref[...] + jnp.dot(vt, p, preferred_element_type=F32)
        m_ref[...] = m_new

    scores(0, s0_ref)

    def pair(p, carry):
        scores(2 * p + 1, s1_ref)
        update(2 * p, s0_ref, False)
        scores(2 * p + 2, s0_ref)
        update(2 * p + 1, s1_ref, False)
        return carry

    lax.fori_loop(0, qi // 2, pair, 0)

    @pl.when(qi % 2 == 0)
    def _():
        update(qi, s0_ref, True)

    @pl.when(qi % 2 == 1)
    def _():
        scores(qi, s1_ref)
        update(qi - 1, s0_ref, False)
        update(qi, s1_ref, True)

    lam = _lambda(lq1_ref, lk1_ref, lq2_ref, lk2_ref, lam_init)
    acc = acc_ref[:HEAD_W, :]
    l = acc_ref[HEAD_W:HEAD_W + 1, :]
    o = acc[:, :t] / l[:, :t] - lam * (acc[:, t:] / l[:, t:])
    o = o * lax.rsqrt(jnp.mean(o * o, axis=0, keepdims=True) + RMS_EPS) * g_ref[...] * (1.0 - lam_init)
    o_ref[0] = o.T.astype(o_ref.dtype)


def _attn_prompt(qb, kb, vtb, lq1, lk1, lq2, lk2, g_col, lam_init, t):
    bsz, seq, _ = qb.shape
    vec = pl.BlockSpec((1, HEAD_DIM_B), lambda b, h, i: (0, 0))
    return pl.pallas_call(
        functools.partial(_attn_kernel, t=t, lam_init=lam_init),
        grid=(bsz, N_HEADS_B, seq // t),
        in_specs=[pl.BlockSpec((1, t, HEAD_W), lambda b, h, i: (b, i, h)),
                  pl.BlockSpec((1, seq, HEAD_W), lambda b, h, i: (b, 0, h)),
                  pl.BlockSpec((1, 1, seq // t, VT_ROWS, t), lambda b, h, i: (b, h, 0, 0, 0)),
                  vec, vec, vec, vec,
                  pl.BlockSpec((HEAD_W, 1), lambda b, h, i: (0, 0))],
        out_specs=pl.BlockSpec((1, t, HEAD_W), lambda b, h, i: (b, i, h)),
        out_shape=jax.ShapeDtypeStruct((bsz, seq, QK_W), BF16),
        scratch_shapes=[pltpu.VMEM((2 * t, HEAD_W), BF16), pltpu.VMEM((t, 2 * t), F32), pltpu.VMEM((t, 2 * t), F32),
                        pltpu.VMEM((1, 2 * t), F32), pltpu.VMEM((VT_ROWS, 2 * t), F32)],
        compiler_params=_cparams(("parallel", "parallel", "arbitrary")),
        name="attn_prompt",
    )(qb, kb, vtb, lq1, lk1, lq2, lk2, g_col)


def _attn_step_kernel(pt_ref, q_ref, kn_ref, vn_ref, lq1_ref, lk1_ref, lq2_ref, lk2_ref, g_ref, *rest,
                      pages, page, lam_init):
    npg = STEP_SEQS * pages
    k_refs = rest[:npg]
    v_refs = rest[npg:2 * npg]
    o_ref, qcol_ref, qblk_ref, m_ref, l_ref, acc_ref = rest[2 * npg:]
    j = pl.program_id(1)
    nmap = 2 * N_HEADS_B
    row16 = lax.broadcasted_iota(jnp.int32, (nmap, page), 0)

    for e in range(STEP_SEQS):
        r = (pl.program_id(0) * STEP_SEQS + e) % 8

        @pl.when(j == 0)
        def _():
            rq = lax.broadcasted_iota(jnp.int32, (nmap, QK_W), 0)
            cq = lax.broadcasted_iota(jnp.int32, (nmap, QK_W), 1)
            qrow = q_ref[pl.ds(r, 1), :]
            qblk_ref[e] = jnp.where(cq // HEAD_DIM_B == rq, jnp.broadcast_to(qrow, (nmap, QK_W)), 0.0)
            qcol_ref[e] = jnp.broadcast_to(qrow, (LANES, QK_W)).T
            m_ref[e] = jnp.full((nmap, 1), -jnp.inf, F32)
            l_ref[e] = jnp.zeros((nmap, 1), F32)
            acc_ref[e] = jnp.zeros((nmap, HEAD_W), F32)

    for e in range(STEP_SEQS):
        s_pages = [jnp.zeros((nmap, page), F32) for _ in range(pages)]
        for c in range(nmap):
            cs = slice(c * HEAD_DIM_B, (c + 1) * HEAD_DIM_B)
            q_c = qcol_ref[e, cs, :]
            for i in range(pages):
                blk = jnp.sum(k_refs[e * pages + i][0, cs, :] * q_c, axis=0, keepdims=True)
                s_pages[i] = jnp.where(row16 == c, jnp.broadcast_to(blk, (nmap, page)), s_pages[i])
        s = jnp.concatenate(s_pages, axis=1)
        m_old = m_ref[e]
        m_new = jnp.maximum(m_old, jnp.max(s, axis=-1, keepdims=True))
        alpha = jnp.exp(m_old - m_new)
        p = jnp.exp(s - m_new)
        l_ref[e] = alpha * l_ref[e] + jnp.sum(p, axis=-1, keepdims=True)
        m_ref[e] = m_new
        pv = jnp.zeros((nmap, HEAD_W), F32)
        for i in range(pages):
            p_i = p[:, i * page:(i + 1) * page]
            for h in range(N_HEADS_B):
                v_h = v_refs[e * pages + i][0, pl.ds(h, page, stride=N_HEADS_B), :]
                p_h = jnp.where(row16 // 2 == h, p_i, 0.0).astype(BF16)
                pv = pv + jnp.dot(p_h, v_h.astype(BF16), preferred_element_type=F32)
        acc_ref[e] = alpha * acc_ref[e] + pv

    for e in range(STEP_SEQS):
        r = (pl.program_id(0) * STEP_SEQS + e) % 8

        @pl.when(j == pl.num_programs(1) - 1)
        def _():
            s_new = jnp.sum(qblk_ref[e] * kn_ref[pl.ds(r, 1), :], axis=-1, keepdims=True)
            m_old = m_ref[e]
            m_new = jnp.maximum(m_old, s_new)
            alpha = jnp.exp(m_old - m_new)
            p_new = jnp.exp(s_new - m_new)
            l = alpha * l_ref[e] + p_new
            vn = vn_ref[pl.ds(r, 1), :]
            rowh = lax.broadcasted_iota(jnp.int32, (nmap, HEAD_W), 0)
            vn_rows = jnp.zeros((nmap, HEAD_W), F32)
            for h in range(N_HEADS_B):
                vn_rows = jnp.where(rowh // 2 == h,
                                    jnp.broadcast_to(vn[:, h * HEAD_W:(h + 1) * HEAD_W], (nmap, HEAD_W)), vn_rows)
            acc = alpha * acc_ref[e] + p_new * vn_rows
            lam = _lambda(lq1_ref, lk1_ref, lq2_ref, lk2_ref, lam_init)
            scaled = acc * (jnp.where(rowh % 2 == 0, 1.0, -lam) / l)
            outs = [_head_rms(scaled[2 * h:2 * h + 1, :] + scaled[2 * h + 1:2 * h + 2, :], g_ref[...], lam_init)
                    for h in range(N_HEADS_B)]
            o_ref[pl.ds(r, 1), :] = jnp.concatenate(outs, axis=1)


def _attn_step(q, k_new, v_new, cache_kt, cache_vf, page_table, lq1, lk1, lq2, lk2, g, lam_init, pages):
    n = q.shape[0]
    n_pages = page_table.shape[1]
    page = cache_kt.shape[2]
    nmap = 2 * N_HEADS_B
    rows = pl.BlockSpec((8, QK_W), lambda b, j, pt: (b * STEP_SEQS // 8, 0))
    vec = pl.BlockSpec((1, HEAD_DIM_B), lambda b, j, pt: (0, 0))

    def page_spec(shape, e, i):
        return pl.BlockSpec((1,) + shape,
                            lambda b, j, pt: (pt[(b * STEP_SEQS + e) * n_pages + j * pages + i], 0, 0))

    slots = [(e, i) for e in range(STEP_SEQS) for i in range(pages)]
    grid_spec = pltpu.PrefetchScalarGridSpec(
        num_scalar_prefetch=1,
        grid=(n // STEP_SEQS, n_pages // pages),
        in_specs=[rows, rows, rows, vec, vec, vec, vec,
                  pl.BlockSpec((1, HEAD_W), lambda b, j, pt: (0, 0))]
                 + [page_spec((QK_W, page), e, i) for e, i in slots]
                 + [page_spec((page * N_HEADS_B, HEAD_W), e, i) for e, i in slots],
        out_specs=rows,
        scratch_shapes=[pltpu.VMEM((STEP_SEQS, QK_W, LANES), F32), pltpu.VMEM((STEP_SEQS, nmap, QK_W), F32),
                        pltpu.VMEM((STEP_SEQS, nmap, 1), F32), pltpu.VMEM((STEP_SEQS, nmap, 1), F32),
                        pltpu.VMEM((STEP_SEQS, nmap, HEAD_W), F32)],
    )
    return pl.pallas_call(
        functools.partial(_attn_step_kernel, pages=pages, page=page, lam_init=lam_init),
        grid_spec=grid_spec,
        out_shape=jax.ShapeDtypeStruct((n, QK_W), F32),
        compiler_params=_cparams(("arbitrary", "arbitrary")),
        name="attn_step",
    )(page_table.reshape(-1), q, k_new, v_new, lq1, lk1, lq2, lk2, g,
      *([cache_kt] * len(slots)), *([cache_vf] * len(slots)))


def _lambda_init(layer):
    return 0.8 - 0.6 * math.exp(-0.3 * layer)


def _row_tile(m, want):
    return want if m % want == 0 else m


def _trunk(x, prm, mixer_a, mixer_b):
    m = x.shape[0]
    tm = _row_tile(m, 512)
    tm_big = _row_tile(m, 1024)
    zx = _proj(x, prm["w_in"], tm_big, D_INNER)
    dtr = _proj(x, prm["w_dt"], tm_big, LANES)
    y, conv_state, ssm_state = mixer_a(zx, dtr)
    x = _proj_ln(y, prm["w_out"], x, prm["ln_g"][0, 0], prm["ln_b"][0, 0], tm)
    x = _mlp(x, prm["w_up"][0], prm["w_down"][0], prm["ln_g"][0, 1], prm["ln_b"][0, 1], tm_big, 1024)
    o, k, v = mixer_b(x)
    x = _proj_ln(o, prm["w_o"], x, prm["ln_g"][1, 0], prm["ln_b"][1, 0], tm)
    x = _mlp(x, prm["w_up"][1], prm["w_down"][1], prm["ln_g"][1, 1], prm["ln_b"][1, 1], tm_big, 1024)
    return x, conv_state, ssm_state, k, v


def kernel(x_prompt, x_sample, state_conv, state_ssm, cache_k, cache_v, page_table, w_in_a, conv_w_a, conv_b_a, dt_bias_a, a_log_a, d_skip_a, norm_a, w_out_a, w_kv, w_q_b, lambda_q1_b, lambda_k1_b, lambda_q2_b, lambda_k2_b, subln_b, w_o_b, w_up, w_down, ln_g, ln_b):
    bp, sp, _ = x_prompt.shape
    bd, ts, _ = x_sample.shape
    assert ts == 1 and sp % ATTN_TILE == 0 and bd % 8 == 0

    w_in = w_in_a[0]
    split = D_INNER + CONV_DIM
    prm = {
        "w_in": w_in[:, :split].astype(BF16),
        "w_dt": jnp.tile(w_in[:, split:], (1, DT_REP)).astype(BF16),
        "w_out": w_out_a[0].astype(BF16),
        "w_kvq": jnp.concatenate([w_kv, w_q_b[0]], axis=1).astype(BF16),
        "w_o": w_o_b[0].astype(BF16),
        "w_up": w_up.astype(BF16),
        "w_down": w_down.astype(BF16),
        "ln_g": ln_g.reshape(DEPTH, 2, 1, D_MODEL),
        "ln_b": ln_b.reshape(DEPTH, 2, 1, D_MODEL),
    }
    cw = conv_w_a[0]
    cb = conv_b_a[0].reshape(1, CONV_DIM)
    dtb = jnp.tile(dt_bias_a[0], DT_REP).reshape(1, LANES)
    alog = jnp.tile(a_log_a[0], DT_REP).reshape(1, LANES)
    dskip = jnp.repeat(d_skip_a[0], HEADDIM_A).reshape(1, D_INNER)
    ng = norm_a[0].reshape(1, D_INNER)
    r128 = _rep_matrix(SSD_CHUNK)
    r64 = _rep_matrix(HEADDIM_A)
    lam_init = _lambda_init(DEPTH // 2)
    lvec = [a[0].reshape(1, HEAD_DIM_B) for a in (lambda_q1_b, lambda_k1_b, lambda_q2_b, lambda_k2_b)]
    subln = subln_b[0]

    def mixer_a_prompt(zx, dtr):
        zx = zx.reshape(bp, sp, ZX_W)
        y, st = _ssd_prompt(zx, dtr.reshape(bp, sp, LANES), bp, sp, cw, cb, dtb, alog, dskip, ng, r128, r64)
        conv_state = zx[:, sp - (CONV_W - 1):, D_INNER:]
        return y.reshape(bp * sp, D_INNER), conv_state, st

    def mixer_b_prompt(x):
        kt, v, kb, vtb, qb = _kvq_prompt(x, prm["w_kvq"], bp, sp, ATTN_TILE)
        shp = (bp, sp, QK_W)
        o = _attn_prompt(qb.reshape(shp), kb.reshape(shp), vtb, *lvec, subln.reshape(HEAD_W, 1), lam_init, ATTN_TILE)
        k = kt.reshape(bp, N_HEADS_B, 2, HEAD_DIM_B, sp).transpose(0, 4, 1, 2, 3)
        return o.reshape(bp * sp, QK_W), k, v.reshape(bp, sp, N_HEADS_B, HEAD_W)

    y_p, conv_p, ssm_p, k_p, v_p = _trunk(x_prompt.reshape(bp * sp, D_MODEL), prm, mixer_a_prompt, mixer_b_prompt)

    n_phys, page = cache_k.shape[:2]
    cache_kt = cache_k.transpose(0, 2, 3, 4, 1).reshape(n_phys, QK_W, page)
    cache_vf = cache_v.reshape(n_phys, page * N_HEADS_B, HEAD_W)

    def mixer_a_sample(zx, dtr):
        y, cst, h = _ssd_step(zx, dtr, state_conv[0].transpose(1, 0, 2), state_ssm[0],
                              cw, cb, dtb, alog, dskip, ng, r64, r128)
        return y, cst.transpose(1, 0, 2), h

    def mixer_b_sample(x):
        k, v, q = _kvq_step(x, prm["w_kvq"], PAST_LEN)
        o = _attn_step(q, k, v, cache_kt, cache_vf, page_table, *lvec, subln.reshape(1, HEAD_W), lam_init,
                       STEP_PAGES)
        return o, k.reshape(bd, 1, N_HEADS_B, 2, HEAD_DIM_B), v.reshape(bd, 1, N_HEADS_B, HEAD_W)

    y_s, conv_s, ssm_s, k_s, v_s = _trunk(x_sample.reshape(bd, D_MODEL), prm, mixer_a_sample, mixer_b_sample)

    return (y_p.reshape(bp, sp, D_MODEL), y_s.reshape(bd, 1, D_MODEL),
            conv_p[None], ssm_p[None], k_p, v_p, conv_s[None], ssm_s[None], k_s, v_s)
```

```python
import functools
import math

import jax
import jax.numpy as jnp
from jax import lax
from jax.experimental import pallas as pl
from jax.experimental.pallas import tpu as pltpu

F32 = jnp.float32
BF16 = jnp.bfloat16

D_MODEL = 1024
DEPTH = 2
PAST_LEN = 2048

D_INNER = 2048
HEADDIM_A = 64
N_HEADS_A = 32
D_STATE = 128
N_GROUPS_A = 8
HEADS_PER_GROUP = N_HEADS_A // N_GROUPS_A
GROUP_W = HEADS_PER_GROUP * HEADDIM_A
CONV_W = 4
CONV_DIM = D_INNER + 2 * N_GROUPS_A * D_STATE
SSD_CHUNK = 128
DT_REP = 4
ZX_W = D_INNER + CONV_DIM

HEAD_DIM_B = 64
N_HEADS_B = 8
HEAD_W = 2 * HEAD_DIM_B
QK_W = N_HEADS_B * HEAD_W
ROPE_THETA = 10000.0
ATTN_TILE = 512
VT_ROWS = HEAD_W + 16
Q_SCALE_LOG2 = HEAD_DIM_B ** -0.5 * math.log2(math.e)

D_FF = 4 * D_MODEL
DN_ALPHA = (2 * DEPTH) ** 0.25
LN_EPS = 1e-5
RMS_EPS = 1e-5

LANES = 128
VMEM_LIMIT = 56 * 1024 * 1024


def _cparams(sem):
    return pltpu.CompilerParams(dimension_semantics=sem, vmem_limit_bytes=VMEM_LIMIT)


def _layer_norm(r, g, b):
    mu = jnp.mean(r, axis=-1, keepdims=True)
    d = r - mu
    var = jnp.mean(d * d, axis=-1, keepdims=True)
    return d * lax.rsqrt(var + LN_EPS) * g + b


def _silu(x):
    h = 0.5 * x
    return h + h * jnp.tanh(h)


def _softplus(x):
    return jnp.maximum(x, 0.0) + jnp.log1p(jnp.exp(-jnp.abs(x)))


def _split3(v):
    hi = v.astype(BF16).astype(F32)
    r1 = v - hi
    mid = r1.astype(BF16).astype(F32)
    lo = r1 - mid
    lane = lax.broadcasted_iota(jnp.int32, v.shape, 1)
    return jnp.where(lane < 32, hi, jnp.where(lane < 64, mid, jnp.where(lane < 96, lo, 0.0))).astype(BF16)


def _rep_matrix(width):
    r = jnp.arange(LANES)[:, None]
    c = jnp.arange(N_HEADS_A * width)[None, :]
    return ((r < 96) & ((r % 32) == (c // width))).astype(BF16)


def _proj_kernel(x_ref, w_ref, o_ref, xb_ref):
    @pl.when(pl.program_id(1) == 0)
    def _():
        xb_ref[...] = x_ref[...].astype(BF16)

    o_ref[...] = jnp.dot(xb_ref[...], w_ref[...], preferred_element_type=F32).astype(o_ref.dtype)


def _proj(x, w, tm, tn):
    m, k = x.shape
    n = w.shape[1]
    return pl.pallas_call(
        _proj_kernel,
        grid=(m // tm, n // tn),
        in_specs=[pl.BlockSpec((tm, k), lambda i, j: (i, 0)),
                  pl.BlockSpec((k, tn), lambda i, j: (0, j))],
        out_specs=pl.BlockSpec((tm, tn), lambda i, j: (i, j)),
        out_shape=jax.ShapeDtypeStruct((m, n), F32),
        scratch_shapes=[pltpu.VMEM((tm, k), BF16)],
        compiler_params=_cparams(("parallel", "arbitrary")),
        name="in_proj",
    )(x, w)


def _proj_ln_kernel(x_ref, w_ref, r_ref, g_ref, b_ref, o_ref):
    y = jnp.dot(x_ref[...].astype(BF16), w_ref[...], preferred_element_type=F32)
    o_ref[...] = _layer_norm(DN_ALPHA * r_ref[...] + y, g_ref[...], b_ref[...])


def _proj_ln(x, w, resid, g, b, tm):
    m, k = x.shape
    n = w.shape[1]
    return pl.pallas_call(
        _proj_ln_kernel,
        grid=(m // tm,),
        in_specs=[pl.BlockSpec((tm, k), lambda i: (i, 0)),
                  pl.BlockSpec((k, n), lambda i: (0, 0)),
                  pl.BlockSpec((tm, n), lambda i: (i, 0)),
                  pl.BlockSpec((1, n), lambda i: (0, 0)),
                  pl.BlockSpec((1, n), lambda i: (0, 0))],
        out_specs=pl.BlockSpec((tm, n), lambda i: (i, 0)),
        out_shape=jax.ShapeDtypeStruct((m, n), F32),
        compiler_params=_cparams(("parallel",)),
        name="proj_ln",
    )(x, w, resid, g, b)


def _mlp_kernel(x_ref, wu_ref, wd_ref, g_ref, b_ref, o_ref, xb_ref, acc_ref):
    j = pl.program_id(1)

    @pl.when(j == 0)
    def _():
        xb_ref[...] = x_ref[...].astype(BF16)
        acc_ref[...] = jnp.zeros_like(acc_ref)

    h = jnp.dot(xb_ref[...], wu_ref[...], preferred_element_type=F32)
    h = jnp.square(jnp.maximum(h, 0.0)).astype(BF16)
    acc_ref[...] += jnp.dot(h, wd_ref[...], preferred_element_type=F32)

    @pl.when(j == pl.num_programs(1) - 1)
    def _():
        o_ref[...] = _layer_norm(DN_ALPHA * x_ref[...] + acc_ref[...], g_ref[...], b_ref[...])


def _mlp(x, w_up, w_down, g, b, tm, tf):
    m, d = x.shape
    ff = w_up.shape[1]
    return pl.pallas_call(
        _mlp_kernel,
        grid=(m // tm, ff // tf),
        in_specs=[pl.BlockSpec((tm, d), lambda i, j: (i, 0)),
                  pl.BlockSpec((d, tf), lambda i, j: (0, j)),
                  pl.BlockSpec((tf, d), lambda i, j: (j, 0)),
                  pl.BlockSpec((1, d), lambda i, j: (0, 0)),
                  pl.BlockSpec((1, d), lambda i, j: (0, 0))],
        out_specs=pl.BlockSpec((tm, d), lambda i, j: (i, 0)),
        out_shape=jax.ShapeDtypeStruct((m, d), F32),
        scratch_shapes=[pltpu.VMEM((tm, d), BF16), pltpu.VMEM((tm, d), F32)],
        compiler_params=_cparams(("parallel", "arbitrary")),
        name="mlp_ln",
    )(x, w_up, w_down, g, b)


def _shift_rows(cur, prev8, k):
    rolled = pltpu.roll(cur, k, 0)
    rolled_prev = pltpu.roll(prev8, k, 0)
    row = lax.broadcasted_iota(jnp.int32, prev8.shape, 0)
    top = jnp.where(row < k, rolled_prev, rolled[:8])
    return jnp.concatenate([top, rolled[8:]], axis=0)


def _ssd_chunk(c, is_last, z_ref, x_ref, bc_ref, dt_ref, cw_ref, cb_ref, dtb_ref, alog_ref, dskip_ref, ng_ref,
               r128_ref, r64_ref, y_ref, st_ref, halo_ref, state_ref):
    q = SSD_CHUNK

    @pl.when(c == 0)
    def _():
        halo_ref[...] = jnp.zeros_like(halo_ref)
        state_ref[...] = jnp.zeros_like(state_ref)

    raw = jnp.concatenate([x_ref[0], bc_ref[0]], axis=1)
    prev8 = halo_ref[...]
    conv = cb_ref[...] + cw_ref[CONV_W - 1:CONV_W, :] * raw
    for k in range(1, CONV_W):
        conv = conv + cw_ref[CONV_W - 1 - k:CONV_W - k, :] * _shift_rows(raw, prev8, k)
    halo_ref[...] = raw[q - 8:, :]
    conv = _silu(conv)
    xc = conv[:, :D_INNER]
    bmat = conv[:, D_INNER:D_INNER + N_GROUPS_A * D_STATE]
    cmat = conv[:, D_INNER + N_GROUPS_A * D_STATE:]

    dt = _softplus(dt_ref[0] + dtb_ref[...])
    da = dt * (-jnp.exp(alog_ref[...]))
    ri = lax.broadcasted_iota(jnp.int32, (q, q), 0)
    ci = lax.broadcasted_iota(jnp.int32, (q, q), 1)
    causal = ri >= ci
    tri = causal.astype(BF16)
    hi = da.astype(BF16)
    r1 = da - hi.astype(F32)
    mid = r1.astype(BF16)
    lo = (r1 - mid.astype(F32)).astype(BF16)
    cs = (jnp.dot(tri, hi, preferred_element_type=F32)
          + jnp.dot(tri, mid, preferred_element_type=F32)
          + jnp.dot(tri, lo, preferred_element_type=F32))
    cs_t = cs.T
    cs_last = cs[q - 1:q, :]

    r64 = r64_ref[...]
    dt_rep = jnp.dot(_split3(dt), r64, preferred_element_type=F32)
    e_rep = jnp.dot(_split3(jnp.exp(cs)), r64, preferred_element_type=F32)
    dte_rep = jnp.dot(_split3(jnp.exp(cs_last - cs)), r64, preferred_element_type=F32)
    cs_rep = jnp.dot(_split3(cs), r128_ref[...], preferred_element_type=F32)

    xs = xc * dt_rep
    xs_b = xs.astype(BF16)
    xe_b = (xs * dte_rep).astype(BF16)
    e_last = e_rep[q - 1:q, :]

    y_groups = []
    for g in range(N_GROUPS_A):
        b_g = bmat[:, g * D_STATE:(g + 1) * D_STATE]
        c_g = cmat[:, g * D_STATE:(g + 1) * D_STATE].astype(BF16)
        b_gt = b_g.T.astype(BF16)
        cb = jnp.dot(c_g, b_gt, preferred_element_type=F32)
        gs = slice(g * GROUP_W, (g + 1) * GROUP_W)
        st_g = state_ref[g]
        y_off = jnp.dot(c_g, st_g.astype(BF16), preferred_element_type=F32) * e_rep[:, gs]
        y_heads = []
        for r in range(HEADS_PER_GROUP):
            h = g * HEADS_PER_GROUP + r
            diff = cs_rep[:, h * q:(h + 1) * q] - cs_t[h:h + 1, :]
            w = (cb * jnp.exp(jnp.where(causal, diff, -jnp.inf))).astype(BF16)
            y_heads.append(jnp.dot(w, xs_b[:, h * HEADDIM_A:(h + 1) * HEADDIM_A],
                                   preferred_element_type=F32))
        y_groups.append(jnp.concatenate(y_heads, axis=1) + y_off)
        state_ref[g] = st_g * e_last[:, gs] + jnp.dot(b_gt, xe_b[:, gs], preferred_element_type=F32)

    y = jnp.concatenate(y_groups, axis=1) + dskip_ref[...] * xc
    y = y * _silu(z_ref[0])
    outs = []
    for g in range(N_GROUPS_A):
        yg = y[:, g * GROUP_W:(g + 1) * GROUP_W]
        outs.append(yg * lax.rsqrt(jnp.mean(yg * yg, axis=-1, keepdims=True) + RMS_EPS))
    y_ref[0] = (jnp.concatenate(outs, axis=1) * ng_ref[...]).astype(y_ref.dtype)

    @pl.when(is_last)
    def _():
        for g in range(N_GROUPS_A):
            st_ref[0, g] = state_ref[g].T


def _ssd_step_pre_kernel(dtr_ref, dtb_ref, alog_ref, r64_ref, r128_ref, dtrep_ref, dabc_ref):
    dt = _softplus(dtr_ref[...] + dtb_ref[...])
    da = jnp.exp(dt * (-jnp.exp(alog_ref[...])))
    dtrep_ref[...] = jnp.dot(_split3(dt), r64_ref[...], preferred_element_type=F32)
    dabc_ref[...] = jnp.dot(_split3(da), r128_ref[...], preferred_element_type=F32)


def _ssd_step_pre(dtr, dtb, alog, r64, r128):
    n = dtr.shape[0]
    full = lambda shape: pl.BlockSpec(shape, lambda i: (0,) * len(shape))
    return pl.pallas_call(
        _ssd_step_pre_kernel,
        grid=(1,),
        in_specs=[full((n, LANES)), full((1, LANES)), full((1, LANES)),
                  full((LANES, D_INNER)), full((LANES, N_HEADS_A * LANES))],
        out_specs=[full((n, D_INNER)), full((n, N_HEADS_A * LANES))],
        out_shape=[jax.ShapeDtypeStruct((n, D_INNER), F32), jax.ShapeDtypeStruct((n, N_HEADS_A * LANES), F32)],
        compiler_params=_cparams(("arbitrary",)),
        name="ssd_step_pre",
    )(dtr, dtb, alog, r64, r128)


def _ssd_step_seq(seq, zx_ref, dtrep_ref, dabc_ref, cst_ref, h_ref, cw_ref, cb_ref, dskip_ref, ng_ref,
                  y_ref, cst_out_ref, h_out_ref):
    r = seq % 8
    zx = zx_ref[pl.ds(r, 1), :]
    raw = zx[:, D_INNER:D_INNER + CONV_DIM]
    conv = cb_ref[...] + cw_ref[CONV_W - 1:CONV_W, :] * raw
    for k in range(CONV_W - 1):
        prev = cst_ref[k, pl.ds(r, 1), :]
        conv = conv + cw_ref[k:k + 1, :] * prev
        if k > 0:
            cst_out_ref[k - 1, pl.ds(r, 1), :] = prev
    cst_out_ref[CONV_W - 2, pl.ds(r, 1), :] = raw
    conv = _silu(conv)
    xc = conv[:, :D_INNER]
    bmat = conv[:, D_INNER:D_INNER + N_GROUPS_A * D_STATE]
    cmat = conv[:, D_INNER + N_GROUPS_A * D_STATE:]

    xdt = xc * dtrep_ref[pl.ds(r, 1), :]
    da = dabc_ref[pl.ds(r, 1), :]

    xdt_col = jnp.broadcast_to(xdt, (LANES, D_INNER)).T

    row8 = lax.broadcasted_iota(jnp.int32, (8, D_STATE), 0)
    c_rows = jnp.zeros((8, D_STATE), F32)
    for g in range(N_GROUPS_A):
        c_rows = jnp.where(row8 == g, jnp.broadcast_to(cmat[:, g * D_STATE:(g + 1) * D_STATE], (8, D_STATE)), c_rows)
    c_rows = c_rows.astype(BF16)

    y_rows = []
    for g in range(N_GROUPS_A):
        gs = slice(g * GROUP_W, (g + 1) * GROUP_W)
        b_g = bmat[:, g * D_STATE:(g + 1) * D_STATE]
        heads = []
        for hh in range(g * HEADS_PER_GROUP, (g + 1) * HEADS_PER_GROUP):
            hs = slice(hh * HEADDIM_A, (hh + 1) * HEADDIM_A)
            heads.append(h_ref[0, hs, :] * da[:, hh * LANES:(hh + 1) * LANES] + xdt_col[hs, :] * b_g)
        h_new = jnp.concatenate(heads, axis=0)
        h_out_ref[0, gs, :] = h_new
        yg = lax.dot_general(c_rows, h_new.astype(BF16), (((1,), (1,)), ((), ())),
                             preferred_element_type=F32)
        y_rows.append(yg[g:g + 1, :])
    y = jnp.concatenate(y_rows, axis=1) + dskip_ref[...] * xc
    y = y * _silu(zx[:, :D_INNER])
    outs = []
    for g in range(N_GROUPS_A):
        yg = y[:, g * GROUP_W:(g + 1) * GROUP_W]
        outs.append(yg * lax.rsqrt(jnp.mean(yg * yg, axis=-1, keepdims=True) + RMS_EPS))
    y_ref[pl.ds(r, 1), :] = jnp.concatenate(outs, axis=1) * ng_ref[...]


N_CHUNK_IN = 12
N_SEQ_IN = 5


def _ssd_both_kernel(*refs, nc):
    chunk_in = refs[:N_CHUNK_IN]
    seq_in = refs[N_CHUNK_IN:N_CHUNK_IN + N_SEQ_IN]
    outs = refs[N_CHUNK_IN + N_SEQ_IN:N_CHUNK_IN + N_SEQ_IN + 5]
    halo_ref, state_ref = refs[N_CHUNK_IN + N_SEQ_IN + 5:]
    c = pl.program_id(1)
    cw_ref, cb_ref, dskip_ref, ng_ref = chunk_in[4], chunk_in[5], chunk_in[8], chunk_in[9]
    _ssd_chunk(c, c == nc - 1, *chunk_in, outs[0], outs[1], halo_ref, state_ref)
    _ssd_step_seq(pl.program_id(0) * nc + c, *seq_in, cw_ref, cb_ref, dskip_ref, ng_ref, *outs[2:])


def _ssd_both(zx, dtr, zx_s, dtr_s, conv_state_t, ssm_state, cw, cb, dtb, alog, dskip, ng, r128, r64):
    bsz, seq, _ = zx.shape
    n = zx_s.shape[0]
    q = SSD_CHUNK
    nc = seq // q
    assert bsz * nc == n, "one sample sequence per prompt chunk step"
    dt_rep, da_bc = _ssd_step_pre(dtr_s, dtb, alog, r64, r128)
    const = lambda shape: pl.BlockSpec(shape, lambda b, c: (0,) * len(shape))
    rows = lambda w: pl.BlockSpec((8, w), lambda b, c: ((b * nc + c) // 8, 0))
    cst_blk = pl.BlockSpec((CONV_W - 1, 8, CONV_DIM), lambda b, c: (0, (b * nc + c) // 8, 0))
    h_blk = pl.BlockSpec((1, D_INNER, D_STATE), lambda b, c: (b * nc + c, 0, 0))
    y, st, y_s, cst, h = pl.pallas_call(
        functools.partial(_ssd_both_kernel, nc=nc),
        grid=(bsz, nc),
        in_specs=[pl.BlockSpec((1, q, D_INNER), lambda b, c: (b, c, 0)),
                  pl.BlockSpec((1, q, D_INNER), lambda b, c: (b, c, 1)),
                  pl.BlockSpec((1, q, D_INNER), lambda b, c: (b, c, 2)),
                  pl.BlockSpec((1, q, LANES), lambda b, c: (b, c, 0)),
                  const((CONV_W, CONV_DIM)), const((1, CONV_DIM)), const((1, LANES)), const((1, LANES)),
                  const((1, D_INNER)), const((1, D_INNER)),
                  const((LANES, N_HEADS_A * q)), const((LANES, D_INNER)),
                  rows(ZX_W), rows(D_INNER), rows(N_HEADS_A * LANES), cst_blk, h_blk],
        out_specs=[pl.BlockSpec((1, q, D_INNER), lambda b, c: (b, c, 0)),
                   pl.BlockSpec((1, N_GROUPS_A, GROUP_W, D_STATE), lambda b, c: (b, 0, 0, 0)),
                   rows(D_INNER), cst_blk, h_blk],
        out_shape=[jax.ShapeDtypeStruct((bsz, seq, D_INNER), BF16),
                   jax.ShapeDtypeStruct((bsz, N_GROUPS_A, GROUP_W, D_STATE), F32),
                   jax.ShapeDtypeStruct((n, D_INNER), F32),
                   jax.ShapeDtypeStruct((CONV_W - 1, n, CONV_DIM), F32),
                   jax.ShapeDtypeStruct((n, D_INNER, D_STATE), F32)],
        scratch_shapes=[pltpu.VMEM((8, CONV_DIM), F32),
                        pltpu.VMEM((N_GROUPS_A, D_STATE, GROUP_W), F32)],
        compiler_params=_cparams(("arbitrary", "arbitrary")),
        name="ssd_both",
    )(zx, zx, zx, dtr, cw, cb, dtb, alog, dskip, ng, r128, r64,
      zx_s, dt_rep, da_bc, conv_state_t, ssm_state.reshape(n, D_INNER, D_STATE))
    return (y, st.reshape(bsz, N_HEADS_A, HEADDIM_A, D_STATE),
            y_s, cst, h.reshape(n, N_HEADS_A, HEADDIM_A, D_STATE))


def _rope_tables(pos_col):
    lane = lax.broadcasted_iota(jnp.int32, pos_col.shape, 1)
    half = HEAD_DIM_B // 2
    idx = (lane % half).astype(F32)
    inv = jnp.exp(idx * (-math.log(ROPE_THETA) / half))
    ang = pos_col * inv
    first = (lane % HEAD_DIM_B) < half
    return jnp.cos(ang), jnp.where(first, -jnp.sin(ang), jnp.sin(ang))


def _rope(x, cos, sin_signed):
    n = x.shape[1]
    reps = n // LANES
    cos_f = jnp.concatenate([cos] * reps, axis=1)
    sin_f = jnp.concatenate([sin_signed] * reps, axis=1)
    half = HEAD_DIM_B // 2
    first = (lax.broadcasted_iota(jnp.int32, x.shape, 1) % HEAD_DIM_B) < half
    rot = jnp.where(first, pltpu.roll(x, n - half, 1), pltpu.roll(x, half, 1))
    return x * cos_f + rot * sin_f


def _kvq_values(x_ref, w_ref, tm, seq, pos0):
    i = pl.program_id(0)
    xb = x_ref[...].astype(BF16)
    row = lax.broadcasted_iota(jnp.int32, (tm, LANES), 0) + i * tm
    pos = (row % seq + pos0).astype(F32)
    cos, sin_signed = _rope_tables(pos)
    k = _rope(jnp.dot(xb, w_ref[:, :QK_W], preferred_element_type=F32), cos, sin_signed)
    v = jnp.dot(xb, w_ref[:, QK_W:2 * QK_W], preferred_element_type=F32)
    q = _rope(jnp.dot(xb, w_ref[:, 2 * QK_W:], preferred_element_type=F32), cos, sin_signed)
    return k, v, q


def _kvq_prompt_kernel(x_ref, w_ref, kt_ref, v_ref, kb_ref, vt_ref, qb_ref, *, tm, seq):
    k, v, q = _kvq_values(x_ref, w_ref, tm, seq, 0)
    kt_ref[0] = k.T
    kb_ref[...] = k.astype(BF16)
    v_ref[...] = v
    vt_ref[0, :, 0, :HEAD_W, :] = v.T.reshape(N_HEADS_B, HEAD_W, tm).astype(BF16)
    vt_ref[0, :, 0, HEAD_W:, :] = jnp.ones((N_HEADS_B, VT_ROWS - HEAD_W, tm), BF16)
    qb_ref[...] = (q * Q_SCALE_LOG2).astype(BF16)


def _kvq_prompt(x, w_kvq, bsz, seq, tm):
    m = x.shape[0]
    nblk = seq // tm
    blk = pl.BlockSpec((tm, QK_W), lambda i: (i, 0))
    return pl.pallas_call(
        functools.partial(_kvq_prompt_kernel, tm=tm, seq=seq),
        grid=(m // tm,),
        in_specs=[pl.BlockSpec((tm, D_MODEL), lambda i: (i, 0)),
                  pl.BlockSpec((D_MODEL, 3 * QK_W), lambda i: (0, 0))],
        out_specs=[pl.BlockSpec((1, QK_W, tm), lambda i: (i // nblk, 0, i % nblk)),
                   blk, blk,
                   pl.BlockSpec((1, N_HEADS_B, 1, VT_ROWS, tm), lambda i: (i // nblk, 0, i % nblk, 0, 0)),
                   blk],
        out_shape=[jax.ShapeDtypeStruct((bsz, QK_W, seq), F32), jax.ShapeDtypeStruct((m, QK_W), F32),
                   jax.ShapeDtypeStruct((m, QK_W), BF16),
                   jax.ShapeDtypeStruct((bsz, N_HEADS_B, nblk, VT_ROWS, tm), BF16),
                   jax.ShapeDtypeStruct((m, QK_W), BF16)],
        compiler_params=_cparams(("parallel",)),
        name="kvq_prompt",
    )(x, w_kvq)


def _kvq_step_kernel(x_ref, w_ref, k_ref, v_ref, q_ref, *, tm, pos0):
    k, v, q = _kvq_values(x_ref, w_ref, tm, 1, pos0)
    k_ref[...] = k
    v_ref[...] = v
    q_ref[...] = q * (HEAD_DIM_B ** -0.5)


def _kvq_step(x, w_kvq, pos0):
    m = x.shape[0]
    blk = pl.BlockSpec((m, QK_W), lambda i: (0, 0))
    return pl.pallas_call(
        functools.partial(_kvq_step_kernel, tm=m, pos0=pos0),
        grid=(1,),
        in_specs=[pl.BlockSpec((m, D_MODEL), lambda i: (0, 0)),
                  pl.BlockSpec((D_MODEL, 3 * QK_W), lambda i: (0, 0))],
        out_specs=[blk, blk, blk],
        out_shape=[jax.ShapeDtypeStruct((m, QK_W), F32)] * 3,
        compiler_params=_cparams(("arbitrary",)),
        name="kvq_step",
    )(x, w_kvq)


def _lambda(lq1_ref, lk1_ref, lq2_ref, lk2_ref, lam_init):
    s1 = jnp.sum(lq1_ref[...] * lk1_ref[...], axis=-1, keepdims=True)
    s2 = jnp.sum(lq2_ref[...] * lk2_ref[...], axis=-1, keepdims=True)
    return jnp.exp(s1) - jnp.exp(s2) + lam_init


def _head_rms(o, g, lam_init):
    return o * lax.rsqrt(jnp.mean(o * o, axis=-1, keepdims=True) + RMS_EPS) * g * (1.0 - lam_init)


def _attn_prompt_tile(qi, q_ref, k_ref, vt_ref, lq1_ref, lk1_ref, lq2_ref, lk2_ref, g_ref, o_ref,
                      qs_ref, s0_ref, s1_ref, m_ref, acc_ref, *, t, lam_init):
    q = q_ref[0].astype(F32)
    lane = lax.broadcasted_iota(jnp.int32, (t, HEAD_W), 1)
    qs_ref[...] = jnp.concatenate([jnp.where(lane < HEAD_DIM_B, q, 0.0),
                                   jnp.where(lane >= HEAD_DIM_B, q, 0.0)], axis=0).astype(BF16)
    m_ref[...] = jnp.full_like(m_ref, -jnp.inf)
    acc_ref[...] = jnp.zeros_like(acc_ref)

    def scores(ki, s_ref):
        start = pl.multiple_of(ki * t, t)
        k = k_ref[0, pl.ds(start, t), :]
        s_ref[...] = lax.dot_general(k, qs_ref[...], (((1,), (1,)), ((), ())),
                                     preferred_element_type=F32)

    def update(ki, s_ref, masked):
        s = s_ref[...]
        if masked:
            key = lax.broadcasted_iota(jnp.int32, (t, 2 * t), 0)
            qry = lax.broadcasted_iota(jnp.int32, (t, 2 * t), 1) % t
            s = jnp.where(key <= qry, s, -jnp.inf)
        m_old = m_ref[...]
        m_new = jnp.maximum(m_old, jnp.max(s, axis=0, keepdims=True))
        p = jnp.exp2(s - m_new).astype(BF16)
        vt = vt_ref[0, 0, ki]
        acc_ref[...] = jnp.exp2(m_old - m_new) * acc_ref[...] + jnp.dot(vt, p, preferred_element_type=F32)
        m_ref[...] = m_new

    scores(0, s0_ref)

    def pair(p, carry):
        scores(2 * p + 1, s1_ref)
        update(2 * p, s0_ref, False)
        scores(2 * p + 2, s0_ref)
        update(2 * p + 1, s1_ref, False)
        return carry

    lax.fori_loop(0, qi // 2, pair, 0)

    @pl.when(qi % 2 == 0)
    def _():
        update(qi, s0_ref, True)

    @pl.when(qi % 2 == 1)
    def _():
        scores(qi, s1_ref)
        update(qi - 1, s0_ref, False)
        update(qi, s1_ref, True)

    lam = _lambda(lq1_ref, lk1_ref, lq2_ref, lk2_ref, lam_init)
    acc = acc_ref[:HEAD_W, :]
    l = acc_ref[HEAD_W:HEAD_W + 1, :]
    o = acc[:, :t] / l[:, :t] - lam * (acc[:, t:] / l[:, t:])
    o = o * lax.rsqrt(jnp.mean(o * o, axis=0, keepdims=True) + RMS_EPS) * g_ref[...] * (1.0 - lam_init)
    o_ref[0] = o.T.astype(o_ref.dtype)


def _attn_step_part(seq, j, is_last, q_ref, kn_ref, vn_ref, lq1_ref, lk1_ref, lq2_ref, lk2_ref, g_ref,
                    k_refs, v_refs, o_ref, qcol_ref, qblk_ref, m_ref, l_ref, acc_ref, *, pages, page, lam_init):
    nmap = 2 * N_HEADS_B
    row16 = lax.broadcasted_iota(jnp.int32, (nmap, page), 0)

    r = seq % 8

    @pl.when(j == 0)
    def _():
        rq = lax.broadcasted_iota(jnp.int32, (nmap, QK_W), 0)
        cq = lax.broadcasted_iota(jnp.int32, (nmap, QK_W), 1)
        qrow = q_ref[pl.ds(r, 1), :]
        qblk_ref[...] = jnp.where(cq // HEAD_DIM_B == rq, jnp.broadcast_to(qrow, (nmap, QK_W)), 0.0)
        qcol_ref[...] = jnp.broadcast_to(qrow, (LANES, QK_W)).T
        m_ref[...] = jnp.full((nmap, 1), -jnp.inf, F32)
        l_ref[...] = jnp.zeros((nmap, 1), F32)
        acc_ref[...] = jnp.zeros((nmap, HEAD_W), F32)

    s_pages = [jnp.zeros((nmap, page), F32) for _ in range(pages)]
    for c in range(nmap):
        cs = slice(c * HEAD_DIM_B, (c + 1) * HEAD_DIM_B)
        q_c = qcol_ref[cs, :]
        for i in range(pages):
            blk = jnp.sum(k_refs[i][0, cs, :] * q_c, axis=0, keepdims=True)
            s_pages[i] = jnp.where(row16 == c, jnp.broadcast_to(blk, (nmap, page)), s_pages[i])
    s = jnp.concatenate(s_pages, axis=1)
    m_old = m_ref[...]
    m_new = jnp.maximum(m_old, jnp.max(s, axis=-1, keepdims=True))
    alpha = jnp.exp(m_old - m_new)
    p = jnp.exp(s - m_new)
    l_ref[...] = alpha * l_ref[...] + jnp.sum(p, axis=-1, keepdims=True)
    m_ref[...] = m_new
    pv = jnp.zeros((nmap, HEAD_W), F32)
    for i in range(pages):
        p_i = p[:, i * page:(i + 1) * page]
        for h in range(N_HEADS_B):
            v_h = v_refs[i][0, pl.ds(h, page, stride=N_HEADS_B), :]
            p_h = jnp.where(row16 // 2 == h, p_i, 0.0).astype(BF16)
            pv = pv + jnp.dot(p_h, v_h.astype(BF16), preferred_element_type=F32)
    acc_ref[...] = alpha * acc_ref[...] + pv

    @pl.when(is_last)
    def _():
        s_new = jnp.sum(qblk_ref[...] * kn_ref[pl.ds(r, 1), :], axis=-1, keepdims=True)
        m_prev = m_ref[...]
        m_fin = jnp.maximum(m_prev, s_new)
        a_fin = jnp.exp(m_prev - m_fin)
        p_new = jnp.exp(s_new - m_fin)
        l = a_fin * l_ref[...] + p_new
        vn = vn_ref[pl.ds(r, 1), :]
        rowh = lax.broadcasted_iota(jnp.int32, (nmap, HEAD_W), 0)
        vn_rows = jnp.zeros((nmap, HEAD_W), F32)
        for h in range(N_HEADS_B):
            vn_rows = jnp.where(rowh // 2 == h,
                                jnp.broadcast_to(vn[:, h * HEAD_W:(h + 1) * HEAD_W], (nmap, HEAD_W)), vn_rows)
        acc = a_fin * acc_ref[...] + p_new * vn_rows
        lam = _lambda(lq1_ref, lk1_ref, lq2_ref, lk2_ref, lam_init)
        scaled = acc * (jnp.where(rowh % 2 == 0, 1.0, -lam) / l)
        outs = [_head_rms(scaled[2 * h:2 * h + 1, :] + scaled[2 * h + 1:2 * h + 2, :], g_ref[...], lam_init)
                for h in range(N_HEADS_B)]
        o_ref[pl.ds(r, 1), :] = jnp.concatenate(outs, axis=1)


N_PROMPT_IN = 8
N_SAMPLE_IN = 4


def _attn_both_kernel(pt_ref, *refs, t, nq, parts, pages, page, lam_init):
    prompt_in = refs[:N_PROMPT_IN]
    sq_ref, kn_ref, vn_ref, grow_ref = refs[N_PROMPT_IN:N_PROMPT_IN + N_SAMPLE_IN]
    base = N_PROMPT_IN + N_SAMPLE_IN
    k_refs = refs[base:base + pages]
    v_refs = refs[base + pages:base + 2 * pages]
    o_ref, so_ref = refs[base + 2 * pages:base + 2 * pages + 2]
    (qs_ref, s0_ref, s1_ref, m_ref, acc_ref,
     qcol_ref, qblk_ref, sm_ref, sl_ref, sacc_ref) = refs[base + 2 * pages + 2:]
    qi = pl.program_id(2)
    step = (pl.program_id(0) * N_HEADS_B + pl.program_id(1)) * nq + qi
    lams = prompt_in[3:7]
    _attn_prompt_tile(qi, *prompt_in, o_ref, qs_ref, s0_ref, s1_ref, m_ref, acc_ref, t=t, lam_init=lam_init)
    _attn_step_part(step // parts, step % parts, step % parts == parts - 1, sq_ref, kn_ref, vn_ref, *lams, grow_ref,
                    k_refs, v_refs, so_ref, qcol_ref, qblk_ref, sm_ref, sl_ref, sacc_ref,
                    pages=pages, page=page, lam_init=lam_init)


def _attn_both(qb, kb, vtb, sq, k_new, v_new, cache_kt, cache_vf, page_table, lq1, lk1, lq2, lk2, g, lam_init, t):
    bsz, seq, _ = qb.shape
    n = sq.shape[0]
    n_pages = page_table.shape[1]
    page = cache_kt.shape[2]
    nq = seq // t
    steps = bsz * N_HEADS_B * nq
    assert steps % n == 0, "every sample sequence gets the same number of grid steps"
    parts = steps // n
    assert n_pages % parts == 0
    pages = n_pages // parts
    nmap = 2 * N_HEADS_B

    def step_of(b, h, i):
        return (b * N_HEADS_B + h) * nq + i

    vec = pl.BlockSpec((1, HEAD_DIM_B), lambda b, h, i, pt: (0, 0))
    rows = pl.BlockSpec((8, QK_W), lambda b, h, i, pt: (step_of(b, h, i) // parts // 8, 0))

    def page_spec(shape, slot):
        def index(b, h, i, pt):
            s = step_of(b, h, i)
            return (pt[(s // parts) * n_pages + (s % parts) * pages + slot], 0, 0)
        return pl.BlockSpec((1,) + shape, index)

    grid_spec = pltpu.PrefetchScalarGridSpec(
        num_scalar_prefetch=1,
        grid=(bsz, N_HEADS_B, nq),
        in_specs=[pl.BlockSpec((1, t, HEAD_W), lambda b, h, i, pt: (b, i, h)),
                  pl.BlockSpec((1, seq, HEAD_W), lambda b, h, i, pt: (b, 0, h)),
                  pl.BlockSpec((1, 1, nq, VT_ROWS, t), lambda b, h, i, pt: (b, h, 0, 0, 0)),
                  vec, vec, vec, vec,
                  pl.BlockSpec((HEAD_W, 1), lambda b, h, i, pt: (0, 0)),
                  rows, rows, rows,
                  pl.BlockSpec((1, HEAD_W), lambda b, h, i, pt: (0, 0))]
                 + [page_spec((QK_W, page), slot) for slot in range(pages)]
                 + [page_spec((page * N_HEADS_B, HEAD_W), slot) for slot in range(pages)],
        out_specs=[pl.BlockSpec((1, t, HEAD_W), lambda b, h, i, pt: (b, i, h)), rows],
        scratch_shapes=[pltpu.VMEM((2 * t, HEAD_W), BF16), pltpu.VMEM((t, 2 * t), F32), pltpu.VMEM((t, 2 * t), F32),
                        pltpu.VMEM((1, 2 * t), F32), pltpu.VMEM((VT_ROWS, 2 * t), F32),
                        pltpu.VMEM((QK_W, LANES), F32), pltpu.VMEM((nmap, QK_W), F32),
                        pltpu.VMEM((nmap, 1), F32), pltpu.VMEM((nmap, 1), F32), pltpu.VMEM((nmap, HEAD_W), F32)],
    )
    return pl.pallas_call(
        functools.partial(_attn_both_kernel, t=t, nq=nq, parts=parts, pages=pages, page=page, lam_init=lam_init),
        grid_spec=grid_spec,
        out_shape=[jax.ShapeDtypeStruct((bsz, seq, QK_W), BF16), jax.ShapeDtypeStruct((n, QK_W), F32)],
        compiler_params=_cparams(("arbitrary", "arbitrary", "arbitrary")),
        name="attn_both",
    )(page_table.reshape(-1), qb, kb, vtb, lq1, lk1, lq2, lk2, g.reshape(HEAD_W, 1),
      sq, k_new, v_new, g.reshape(1, HEAD_W), *([cache_kt] * pages), *([cache_vf] * pages))


def _lambda_init(layer):
    return 0.8 - 0.6 * math.exp(-0.3 * layer)


def _row_tile(m, want):
    return want if m % want == 0 else m


def _in_proj(x, prm):
    tm = _row_tile(x.shape[0], 1024)
    return _proj(x, prm["w_in"], tm, D_INNER), _proj(x, prm["w_dt"], tm, LANES)


def _layer0_tail(x, y, prm):
    m = x.shape[0]
    x = _proj_ln(y, prm["w_out"], x, prm["ln_g"][0, 0], prm["ln_b"][0, 0], _row_tile(m, 512))
    return _mlp(x, prm["w_up"][0], prm["w_down"][0], prm["ln_g"][0, 1], prm["ln_b"][0, 1], _row_tile(m, 1024), 1024)


def _layer1_tail(x, o, prm):
    m = x.shape[0]
    x = _proj_ln(o, prm["w_o"], x, prm["ln_g"][1, 0], prm["ln_b"][1, 0], _row_tile(m, 512))
    return _mlp(x, prm["w_up"][1], prm["w_down"][1], prm["ln_g"][1, 1], prm["ln_b"][1, 1], _row_tile(m, 1024), 1024)


def kernel(x_prompt, x_sample, state_conv, state_ssm, cache_k, cache_v, page_table, w_in_a, conv_w_a, conv_b_a, dt_bias_a, a_log_a, d_skip_a, norm_a, w_out_a, w_kv, w_q_b, lambda_q1_b, lambda_k1_b, lambda_q2_b, lambda_k2_b, subln_b, w_o_b, w_up, w_down, ln_g, ln_b):
    bp, sp, _ = x_prompt.shape
    bd, ts, _ = x_sample.shape
    assert ts == 1 and sp % ATTN_TILE == 0 and bd % 8 == 0

    w_in = w_in_a[0]
    split = D_INNER + CONV_DIM
    prm = {
        "w_in": w_in[:, :split].astype(BF16),
        "w_dt": jnp.tile(w_in[:, split:], (1, DT_REP)).astype(BF16),
        "w_out": w_out_a[0].astype(BF16),
        "w_kvq": jnp.concatenate([w_kv, w_q_b[0]], axis=1).astype(BF16),
        "w_o": w_o_b[0].astype(BF16),
        "w_up": w_up.astype(BF16),
        "w_down": w_down.astype(BF16),
        "ln_g": ln_g.reshape(DEPTH, 2, 1, D_MODEL),
        "ln_b": ln_b.reshape(DEPTH, 2, 1, D_MODEL),
    }
    cw = conv_w_a[0]
    cb = conv_b_a[0].reshape(1, CONV_DIM)
    dtb = jnp.tile(dt_bias_a[0], DT_REP).reshape(1, LANES)
    alog = jnp.tile(a_log_a[0], DT_REP).reshape(1, LANES)
    dskip = jnp.repeat(d_skip_a[0], HEADDIM_A).reshape(1, D_INNER)
    ng = norm_a[0].reshape(1, D_INNER)
    r128 = _rep_matrix(SSD_CHUNK)
    r64 = _rep_matrix(HEADDIM_A)
    lam_init = _lambda_init(DEPTH // 2)
    lvec = [a[0].reshape(1, HEAD_DIM_B) for a in (lambda_q1_b, lambda_k1_b, lambda_q2_b, lambda_k2_b)]

    xp0 = x_prompt.reshape(bp * sp, D_MODEL)
    xs0 = x_sample.reshape(bd, D_MODEL)
    zx_p, dtr_p = _in_proj(xp0, prm)
    zx_s, dtr_s = _in_proj(xs0, prm)
    zx_p = zx_p.reshape(bp, sp, ZX_W)
    ya_p, ssm_p, ya_s, cst_s, ssm_s = _ssd_both(zx_p, dtr_p.reshape(bp, sp, LANES), zx_s, dtr_s,
                                                state_conv[0].transpose(1, 0, 2), state_ssm[0],
                                                cw, cb, dtb, alog, dskip, ng, r128, r64)
    conv_p = zx_p[:, sp - (CONV_W - 1):, D_INNER:]
    conv_s = cst_s.transpose(1, 0, 2)
    x_p = _layer0_tail(xp0, ya_p.reshape(bp * sp, D_INNER), prm)
    x_s = _layer0_tail(xs0, ya_s, prm)

    kt_p, v_p, kb, vtb, qb = _kvq_prompt(x_p, prm["w_kvq"], bp, sp, ATTN_TILE)
    k_s, v_s, q_s = _kvq_step(x_s, prm["w_kvq"], PAST_LEN)

    n_phys, page = cache_k.shape[:2]
    cache_kt = cache_k.transpose(0, 2, 3, 4, 1).reshape(n_phys, QK_W, page)
    cache_vf = cache_v.reshape(n_phys, page * N_HEADS_B, HEAD_W)
    shp = (bp, sp, QK_W)
    o_p, o_s = _attn_both(qb.reshape(shp), kb.reshape(shp), vtb, q_s, k_s, v_s, cache_kt, cache_vf, page_table,
                          *lvec, subln_b[0], lam_init, ATTN_TILE)

    y_p = _layer1_tail(x_p, o_p.reshape(bp * sp, QK_W), prm)
    y_s = _layer1_tail(x_s, o_s, prm)

    k_p = kt_p.reshape(bp, N_HEADS_B, 2, HEAD_DIM_B, sp).transpose(0, 4, 1, 2, 3)
    return (y_p.reshape(bp, sp, D_MODEL), y_s.reshape(bd, 1, D_MODEL),
            conv_p[None], ssm_p[None], k_p, v_p.reshape(bp, sp, N_HEADS_B, HEAD_W),
            conv_s[None], ssm_s[None],
            k_s.reshape(bd, 1, N_HEADS_B, 2, HEAD_DIM_B), v_s.reshape(bd, 1, N_HEADS_B, HEAD_W))
```

```python
import functools
import math

import jax
import jax.numpy as jnp
from jax import lax
from jax.experimental import pallas as pl
from jax.experimental.pallas import tpu as pltpu

F32 = jnp.float32
BF16 = jnp.bfloat16

D_MODEL = 1024
DEPTH = 2
PAST_LEN = 2048

D_INNER = 2048
HEADDIM_A = 64
N_HEADS_A = 32
D_STATE = 128
N_GROUPS_A = 8
HEADS_PER_GROUP = N_HEADS_A // N_GROUPS_A
GROUP_W = HEADS_PER_GROUP * HEADDIM_A
CONV_W = 4
CONV_DIM = D_INNER + 2 * N_GROUPS_A * D_STATE
SSD_CHUNK = 128
DT_REP = 4
ZX_W = D_INNER + CONV_DIM

HEAD_DIM_B = 64
N_HEADS_B = 8
HEAD_W = 2 * HEAD_DIM_B
QK_W = N_HEADS_B * HEAD_W
ROPE_THETA = 10000.0
ATTN_TILE = 512
VT_ROWS = HEAD_W + 16
Q_SCALE_LOG2 = HEAD_DIM_B ** -0.5 * math.log2(math.e)

D_FF = 4 * D_MODEL
DN_ALPHA = (2 * DEPTH) ** 0.25
LN_EPS = 1e-5
RMS_EPS = 1e-5

LANES = 128
VMEM_LIMIT = 56 * 1024 * 1024


def _cparams(sem):
    return pltpu.CompilerParams(dimension_semantics=sem, vmem_limit_bytes=VMEM_LIMIT)


def _layer_norm(r, g, b):
    mu = jnp.mean(r, axis=-1, keepdims=True)
    d = r - mu
    var = jnp.mean(d * d, axis=-1, keepdims=True)
    return d * lax.rsqrt(var + LN_EPS) * g + b


def _silu(x):
    h = 0.5 * x
    return h + h * jnp.tanh(h)


def _softplus(x):
    return jnp.maximum(x, 0.0) + jnp.log1p(jnp.exp(-jnp.abs(x)))


def _split3(v):
    hi = v.astype(BF16).astype(F32)
    r1 = v - hi
    mid = r1.astype(BF16).astype(F32)
    lo = r1 - mid
    lane = lax.broadcasted_iota(jnp.int32, v.shape, 1)
    return jnp.where(lane < 32, hi, jnp.where(lane < 64, mid, jnp.where(lane < 96, lo, 0.0))).astype(BF16)


def _rep_matrix(width):
    r = jnp.arange(LANES)[:, None]
    c = jnp.arange(N_HEADS_A * width)[None, :]
    return ((r < 96) & ((r % 32) == (c // width))).astype(BF16)


def _proj_kernel(x_ref, w_ref, o_ref, xb_ref):
    @pl.when(pl.program_id(1) == 0)
    def _():
        xb_ref[...] = x_ref[...].astype(BF16)

    o_ref[...] = jnp.dot(xb_ref[...], w_ref[...], preferred_element_type=F32).astype(o_ref.dtype)


def _proj(x, w, tm, tn):
    m, k = x.shape
    n = w.shape[1]
    return pl.pallas_call(
        _proj_kernel,
        grid=(m // tm, n // tn),
        in_specs=[pl.BlockSpec((tm, k), lambda i, j: (i, 0)),
                  pl.BlockSpec((k, tn), lambda i, j: (0, j))],
        out_specs=pl.BlockSpec((tm, tn), lambda i, j: (i, j)),
        out_shape=jax.ShapeDtypeStruct((m, n), F32),
        scratch_shapes=[pltpu.VMEM((tm, k), BF16)],
        compiler_params=_cparams(("parallel", "arbitrary")),
        name="in_proj",
    )(x, w)


def _proj_ln_kernel(x_ref, w_ref, r_ref, g_ref, b_ref, o_ref):
    y = jnp.dot(x_ref[...].astype(BF16), w_ref[...], preferred_element_type=F32)
    o_ref[...] = _layer_norm(DN_ALPHA * r_ref[...] + y, g_ref[...], b_ref[...])


def _proj_ln(x, w, resid, g, b, tm):
    m, k = x.shape
    n = w.shape[1]
    return pl.pallas_call(
        _proj_ln_kernel,
        grid=(m // tm,),
        in_specs=[pl.BlockSpec((tm, k), lambda i: (i, 0)),
                  pl.BlockSpec((k, n), lambda i: (0, 0)),
                  pl.BlockSpec((tm, n), lambda i: (i, 0)),
                  pl.BlockSpec((1, n), lambda i: (0, 0)),
                  pl.BlockSpec((1, n), lambda i: (0, 0))],
        out_specs=pl.BlockSpec((tm, n), lambda i: (i, 0)),
        out_shape=jax.ShapeDtypeStruct((m, n), F32),
        compiler_params=_cparams(("parallel",)),
        name="proj_ln",
    )(x, w, resid, g, b)


def _mlp_kernel(x_ref, wu_ref, wd_ref, g_ref, b_ref, o_ref, xb_ref, acc_ref):
    j = pl.program_id(1)

    @pl.when(j == 0)
    def _():
        xb_ref[...] = x_ref[...].astype(BF16)
        acc_ref[...] = jnp.zeros_like(acc_ref)

    h = jnp.dot(xb_ref[...], wu_ref[...], preferred_element_type=F32)
    h = jnp.square(jnp.maximum(h, 0.0)).astype(BF16)
    acc_ref[...] += jnp.dot(h, wd_ref[...], preferred_element_type=F32)

    @pl.when(j == pl.num_programs(1) - 1)
    def _():
        o_ref[...] = _layer_norm(DN_ALPHA * x_ref[...] + acc_ref[...], g_ref[...], b_ref[...])


def _mlp(x, w_up, w_down, g, b, tm, tf):
    m, d = x.shape
    ff = w_up.shape[1]
    return pl.pallas_call(
        _mlp_kernel,
        grid=(m // tm, ff // tf),
        in_specs=[pl.BlockSpec((tm, d), lambda i, j: (i, 0)),
                  pl.BlockSpec((d, tf), lambda i, j: (0, j)),
                  pl.BlockSpec((tf, d), lambda i, j: (j, 0)),
                  pl.BlockSpec((1, d), lambda i, j: (0, 0)),
                  pl.BlockSpec((1, d), lambda i, j: (0, 0))],
        out_specs=pl.BlockSpec((tm, d), lambda i, j: (i, 0)),
        out_shape=jax.ShapeDtypeStruct((m, d), F32),
        scratch_shapes=[pltpu.VMEM((tm, d), BF16), pltpu.VMEM((tm, d), F32)],
        compiler_params=_cparams(("parallel", "arbitrary")),
        name="mlp_ln",
    )(x, w_up, w_down, g, b)


def _shift_rows(cur, prev8, k):
    rolled = pltpu.roll(cur, k, 0)
    rolled_prev = pltpu.roll(prev8, k, 0)
    row = lax.broadcasted_iota(jnp.int32, prev8.shape, 0)
    top = jnp.where(row < k, rolled_prev, rolled[:8])
    return jnp.concatenate([top, rolled[8:]], axis=0)


def _ssd_chunk(c, is_last, alongside, z_ref, x_ref, bc_ref, dt_ref, cw_ref, cb_ref, dtb_ref, alog_ref, dskip_ref,
               ng_ref, r128_ref, r64_ref, y_ref, st_ref, halo_ref, state_ref):
    q = SSD_CHUNK

    @pl.when(c == 0)
    def _():
        halo_ref[...] = jnp.zeros_like(halo_ref)
        state_ref[...] = jnp.zeros_like(state_ref)

    raw = jnp.concatenate([x_ref[0], bc_ref[0]], axis=1)
    prev8 = halo_ref[...]
    conv = cb_ref[...] + cw_ref[CONV_W - 1:CONV_W, :] * raw
    for k in range(1, CONV_W):
        conv = conv + cw_ref[CONV_W - 1 - k:CONV_W - k, :] * _shift_rows(raw, prev8, k)
    halo_ref[...] = raw[q - 8:, :]
    conv = _silu(conv)
    xc = conv[:, :D_INNER]
    bmat = conv[:, D_INNER:D_INNER + N_GROUPS_A * D_STATE]
    cmat = conv[:, D_INNER + N_GROUPS_A * D_STATE:]

    dt = _softplus(dt_ref[0] + dtb_ref[...])
    da = dt * (-jnp.exp(alog_ref[...]))
    ri = lax.broadcasted_iota(jnp.int32, (q, q), 0)
    ci = lax.broadcasted_iota(jnp.int32, (q, q), 1)
    causal = ri >= ci
    tri = causal.astype(BF16)
    hi = da.astype(BF16)
    r1 = da - hi.astype(F32)
    mid = r1.astype(BF16)
    lo = (r1 - mid.astype(F32)).astype(BF16)
    cs = (jnp.dot(tri, hi, preferred_element_type=F32)
          + jnp.dot(tri, mid, preferred_element_type=F32)
          + jnp.dot(tri, lo, preferred_element_type=F32))
    cs_t = cs.T
    cs_last = cs[q - 1:q, :]

    r64 = r64_ref[...]
    dt_rep = jnp.dot(_split3(dt), r64, preferred_element_type=F32)
    e_rep = jnp.dot(_split3(jnp.exp(cs)), r64, preferred_element_type=F32)
    dte_rep = jnp.dot(_split3(jnp.exp(cs_last - cs)), r64, preferred_element_type=F32)
    cs_rep = jnp.dot(_split3(cs), r128_ref[...], preferred_element_type=F32)

    xs = xc * dt_rep
    xs_b = xs.astype(BF16)
    xe_b = (xs * dte_rep).astype(BF16)
    e_last = e_rep[q - 1:q, :]

    y_groups = []
    for g in range(N_GROUPS_A):
        b_g = bmat[:, g * D_STATE:(g + 1) * D_STATE]
        c_g = cmat[:, g * D_STATE:(g + 1) * D_STATE].astype(BF16)
        b_gt = b_g.T.astype(BF16)
        cb = jnp.dot(c_g, b_gt, preferred_element_type=F32)
        gs = slice(g * GROUP_W, (g + 1) * GROUP_W)
        st_g = state_ref[g]
        y_off = jnp.dot(c_g, st_g.astype(BF16), preferred_element_type=F32) * e_rep[:, gs]
        y_heads = []
        for r in range(HEADS_PER_GROUP):
            h = g * HEADS_PER_GROUP + r
            diff = cs_rep[:, h * q:(h + 1) * q] - cs_t[h:h + 1, :]
            w = (cb * jnp.exp(jnp.where(causal, diff, -jnp.inf))).astype(BF16)
            y_heads.append(jnp.dot(w, xs_b[:, h * HEADDIM_A:(h + 1) * HEADDIM_A],
                                   preferred_element_type=F32))
        y_groups.append(jnp.concatenate(y_heads, axis=1) + y_off)
        state_ref[g] = st_g * e_last[:, gs] + jnp.dot(b_gt, xe_b[:, gs], preferred_element_type=F32)

    y = jnp.concatenate(y_groups, axis=1) + dskip_ref[...] * xc
    y = y * _silu(z_ref[0])
    outs = []
    for g in range(N_GROUPS_A):
        yg = y[:, g * GROUP_W:(g + 1) * GROUP_W]
        outs.append(yg * lax.rsqrt(jnp.mean(yg * yg, axis=-1, keepdims=True) + RMS_EPS))
    y_ref[0] = (jnp.concatenate(outs, axis=1) * ng_ref[...]).astype(y_ref.dtype)
    alongside()

    @pl.when(is_last)
    def _():
        for g in range(N_GROUPS_A):
            st_ref[0, g] = state_ref[g].T


def _ssd_step_pre_kernel(dtr_ref, dtb_ref, alog_ref, r64_ref, r128_ref, dtrep_ref, dabc_ref):
    dt = _softplus(dtr_ref[...] + dtb_ref[...])
    da = jnp.exp(dt * (-jnp.exp(alog_ref[...])))
    dtrep_ref[...] = jnp.dot(_split3(dt), r64_ref[...], preferred_element_type=F32)
    dabc_ref[...] = jnp.dot(_split3(da), r128_ref[...], preferred_element_type=F32)


def _ssd_step_pre(dtr, dtb, alog, r64, r128):
    n = dtr.shape[0]
    full = lambda shape: pl.BlockSpec(shape, lambda i: (0,) * len(shape))
    return pl.pallas_call(
        _ssd_step_pre_kernel,
        grid=(1,),
        in_specs=[full((n, LANES)), full((1, LANES)), full((1, LANES)),
                  full((LANES, D_INNER)), full((LANES, N_HEADS_A * LANES))],
        out_specs=[full((n, D_INNER)), full((n, N_HEADS_A * LANES))],
        out_shape=[jax.ShapeDtypeStruct((n, D_INNER), F32), jax.ShapeDtypeStruct((n, N_HEADS_A * LANES), F32)],
        compiler_params=_cparams(("arbitrary",)),
        name="ssd_step_pre",
    )(dtr, dtb, alog, r64, r128)


def _ssd_step_seq(seq, zx_ref, dtrep_ref, dabc_ref, cst_ref, h_ref, cw_ref, cb_ref, dskip_ref, ng_ref,
                  y_ref, cst_out_ref, h_out_ref):
    r = seq % 8
    zx = zx_ref[pl.ds(r, 1), :]
    raw = zx[:, D_INNER:D_INNER + CONV_DIM]
    conv = cb_ref[...] + cw_ref[CONV_W - 1:CONV_W, :] * raw
    for k in range(CONV_W - 1):
        prev = cst_ref[k, pl.ds(r, 1), :]
        conv = conv + cw_ref[k:k + 1, :] * prev
        if k > 0:
            cst_out_ref[k - 1, pl.ds(r, 1), :] = prev
    cst_out_ref[CONV_W - 2, pl.ds(r, 1), :] = raw
    conv = _silu(conv)
    xc = conv[:, :D_INNER]
    bmat = conv[:, D_INNER:D_INNER + N_GROUPS_A * D_STATE]
    cmat = conv[:, D_INNER + N_GROUPS_A * D_STATE:]

    xdt = xc * dtrep_ref[pl.ds(r, 1), :]
    da = dabc_ref[pl.ds(r, 1), :]

    xdt_col = jnp.broadcast_to(xdt, (LANES, D_INNER)).T

    row8 = lax.broadcasted_iota(jnp.int32, (8, D_STATE), 0)
    c_rows = jnp.zeros((8, D_STATE), F32)
    for g in range(N_GROUPS_A):
        c_rows = jnp.where(row8 == g, jnp.broadcast_to(cmat[:, g * D_STATE:(g + 1) * D_STATE], (8, D_STATE)), c_rows)
    c_rows = c_rows.astype(BF16)

    y_rows = []
    for g in range(N_GROUPS_A):
        gs = slice(g * GROUP_W, (g + 1) * GROUP_W)
        b_g = bmat[:, g * D_STATE:(g + 1) * D_STATE]
        heads = []
        for hh in range(g * HEADS_PER_GROUP, (g + 1) * HEADS_PER_GROUP):
            hs = slice(hh * HEADDIM_A, (hh + 1) * HEADDIM_A)
            heads.append(h_ref[0, hs, :] * da[:, hh * LANES:(hh + 1) * LANES] + xdt_col[hs, :] * b_g)
        h_new = jnp.concatenate(heads, axis=0)
        h_out_ref[0, gs, :] = h_new
        yg = lax.dot_general(c_rows, h_new.astype(BF16), (((1,), (1,)), ((), ())),
                             preferred_element_type=F32)
        y_rows.append(yg[g:g + 1, :])
    y = jnp.concatenate(y_rows, axis=1) + dskip_ref[...] * xc
    y = y * _silu(zx[:, :D_INNER])
    outs = []
    for g in range(N_GROUPS_A):
        yg = y[:, g * GROUP_W:(g + 1) * GROUP_W]
        outs.append(yg * lax.rsqrt(jnp.mean(yg * yg, axis=-1, keepdims=True) + RMS_EPS))
    y_ref[pl.ds(r, 1), :] = jnp.concatenate(outs, axis=1) * ng_ref[...]


N_CHUNK_IN = 12
N_SEQ_IN = 5


def _ssd_both_kernel(*refs, nc):
    chunk_in = refs[:N_CHUNK_IN]
    seq_in = refs[N_CHUNK_IN:N_CHUNK_IN + N_SEQ_IN]
    outs = refs[N_CHUNK_IN + N_SEQ_IN:N_CHUNK_IN + N_SEQ_IN + 5]
    halo_ref, state_ref = refs[N_CHUNK_IN + N_SEQ_IN + 5:]
    c = pl.program_id(1)
    cw_ref, cb_ref, dskip_ref, ng_ref = chunk_in[4], chunk_in[5], chunk_in[8], chunk_in[9]
    step_seq = functools.partial(_ssd_step_seq, pl.program_id(0) * nc + c, *seq_in, cw_ref, cb_ref, dskip_ref, ng_ref,
                                 *outs[2:])
    _ssd_chunk(c, c == nc - 1, step_seq, *chunk_in, outs[0], outs[1], halo_ref, state_ref)


def _ssd_both(zx, dtr, zx_s, dtr_s, conv_state_t, ssm_state, cw, cb, dtb, alog, dskip, ng, r128, r64):
    bsz, seq, _ = zx.shape
    n = zx_s.shape[0]
    q = SSD_CHUNK
    nc = seq // q
    assert bsz * nc == n, "one sample sequence per prompt chunk step"
    dt_rep, da_bc = _ssd_step_pre(dtr_s, dtb, alog, r64, r128)
    const = lambda shape: pl.BlockSpec(shape, lambda b, c: (0,) * len(shape))
    rows = lambda w: pl.BlockSpec((8, w), lambda b, c: ((b * nc + c) // 8, 0))
    cst_blk = pl.BlockSpec((CONV_W - 1, 8, CONV_DIM), lambda b, c: (0, (b * nc + c) // 8, 0))
    h_blk = pl.BlockSpec((1, D_INNER, D_STATE), lambda b, c: (b * nc + c, 0, 0))
    y, st, y_s, cst, h = pl.pallas_call(
        functools.partial(_ssd_both_kernel, nc=nc),
        grid=(bsz, nc),
        in_specs=[pl.BlockSpec((1, q, D_INNER), lambda b, c: (b, c, 0)),
                  pl.BlockSpec((1, q, D_INNER), lambda b, c: (b, c, 1)),
                  pl.BlockSpec((1, q, D_INNER), lambda b, c: (b, c, 2)),
                  pl.BlockSpec((1, q, LANES), lambda b, c: (b, c, 0)),
                  const((CONV_W, CONV_DIM)), const((1, CONV_DIM)), const((1, LANES)), const((1, LANES)),
                  const((1, D_INNER)), const((1, D_INNER)),
                  const((LANES, N_HEADS_A * q)), const((LANES, D_INNER)),
                  rows(ZX_W), rows(D_INNER), rows(N_HEADS_A * LANES), cst_blk, h_blk],
        out_specs=[pl.BlockSpec((1, q, D_INNER), lambda b, c: (b, c, 0)),
                   pl.BlockSpec((1, N_GROUPS_A, GROUP_W, D_STATE), lambda b, c: (b, 0, 0, 0)),
                   rows(D_INNER), cst_blk, h_blk],
        out_shape=[jax.ShapeDtypeStruct((bsz, seq, D_INNER), BF16),
                   jax.ShapeDtypeStruct((bsz, N_GROUPS_A, GROUP_W, D_STATE), F32),
                   jax.ShapeDtypeStruct((n, D_INNER), F32),
                   jax.ShapeDtypeStruct((CONV_W - 1, n, CONV_DIM), F32),
                   jax.ShapeDtypeStruct((n, D_INNER, D_STATE), F32)],
        scratch_shapes=[pltpu.VMEM((8, CONV_DIM), F32),
                        pltpu.VMEM((N_GROUPS_A, D_STATE, GROUP_W), F32)],
        compiler_params=_cparams(("arbitrary", "arbitrary")),
        name="ssd_both",
    )(zx, zx, zx, dtr, cw, cb, dtb, alog, dskip, ng, r128, r64,
      zx_s, dt_rep, da_bc, conv_state_t, ssm_state.reshape(n, D_INNER, D_STATE))
    return (y, st.reshape(bsz, N_HEADS_A, HEADDIM_A, D_STATE),
            y_s, cst, h.reshape(n, N_HEADS_A, HEADDIM_A, D_STATE))


def _rope_tables(pos_col):
    lane = lax.broadcasted_iota(jnp.int32, pos_col.shape, 1)
    half = HEAD_DIM_B // 2
    idx = (lane % half).astype(F32)
    inv = jnp.exp(idx * (-math.log(ROPE_THETA) / half))
    ang = pos_col * inv
    first = (lane % HEAD_DIM_B) < half
    return jnp.cos(ang), jnp.where(first, -jnp.sin(ang), jnp.sin(ang))


def _rope(x, cos, sin_signed):
    n = x.shape[1]
    reps = n // LANES
    cos_f = jnp.concatenate([cos] * reps, axis=1)
    sin_f = jnp.concatenate([sin_signed] * reps, axis=1)
    half = HEAD_DIM_B // 2
    first = (lax.broadcasted_iota(jnp.int32, x.shape, 1) % HEAD_DIM_B) < half
    rot = jnp.where(first, pltpu.roll(x, n - half, 1), pltpu.roll(x, half, 1))
    return x * cos_f + rot * sin_f


def _kvq_values(x_ref, w_ref, tm, seq, pos0):
    i = pl.program_id(0)
    xb = x_ref[...].astype(BF16)
    row = lax.broadcasted_iota(jnp.int32, (tm, LANES), 0) + i * tm
    pos = (row % seq + pos0).astype(F32)
    cos, sin_signed = _rope_tables(pos)
    k = _rope(jnp.dot(xb, w_ref[:, :QK_W], preferred_element_type=F32), cos, sin_signed)
    v = jnp.dot(xb, w_ref[:, QK_W:2 * QK_W], preferred_element_type=F32)
    q = _rope(jnp.dot(xb, w_ref[:, 2 * QK_W:], preferred_element_type=F32), cos, sin_signed)
    return k, v, q


def _kvq_prompt_kernel(x_ref, w_ref, kt_ref, v_ref, kb_ref, vt_ref, qb_ref, *, tm, seq):
    k, v, q = _kvq_values(x_ref, w_ref, tm, seq, 0)
    kt_ref[0] = k.T
    kb_ref[...] = k.astype(BF16)
    v_ref[...] = v
    vt_ref[0, :, 0, :HEAD_W, :] = v.T.reshape(N_HEADS_B, HEAD_W, tm).astype(BF16)
    vt_ref[0, :, 0, HEAD_W:, :] = jnp.ones((N_HEADS_B, VT_ROWS - HEAD_W, tm), BF16)
    qb_ref[...] = (q * Q_SCALE_LOG2).astype(BF16)


def _kvq_prompt(x, w_kvq, bsz, seq, tm):
    m = x.shape[0]
    nblk = seq // tm
    blk = pl.BlockSpec((tm, QK_W), lambda i: (i, 0))
    return pl.pallas_call(
        functools.partial(_kvq_prompt_kernel, tm=tm, seq=seq),
        grid=(m // tm,),
        in_specs=[pl.BlockSpec((tm, D_MODEL), lambda i: (i, 0)),
                  pl.BlockSpec((D_MODEL, 3 * QK_W), lambda i: (0, 0))],
        out_specs=[pl.BlockSpec((1, QK_W, tm), lambda i: (i // nblk, 0, i % nblk)),
                   blk, blk,
                   pl.BlockSpec((1, N_HEADS_B, 1, VT_ROWS, tm), lambda i: (i // nblk, 0, i % nblk, 0, 0)),
                   blk],
        out_shape=[jax.ShapeDtypeStruct((bsz, QK_W, seq), F32), jax.ShapeDtypeStruct((m, QK_W), F32),
                   jax.ShapeDtypeStruct((m, QK_W), BF16),
                   jax.ShapeDtypeStruct((bsz, N_HEADS_B, nblk, VT_ROWS, tm), BF16),
                   jax.ShapeDtypeStruct((m, QK_W), BF16)],
        compiler_params=_cparams(("parallel",)),
        name="kvq_prompt",
    )(x, w_kvq)


def _kvq_step_kernel(x_ref, w_ref, k_ref, v_ref, q_ref, *, tm, pos0):
    k, v, q = _kvq_values(x_ref, w_ref, tm, 1, pos0)
    k_ref[...] = k
    v_ref[...] = v
    q_ref[...] = q * (HEAD_DIM_B ** -0.5)


def _kvq_step(x, w_kvq, pos0):
    m = x.shape[0]
    blk = pl.BlockSpec((m, QK_W), lambda i: (0, 0))
    return pl.pallas_call(
        functools.partial(_kvq_step_kernel, tm=m, pos0=pos0),
        grid=(1,),
        in_specs=[pl.BlockSpec((m, D_MODEL), lambda i: (0, 0)),
                  pl.BlockSpec((D_MODEL, 3 * QK_W), lambda i: (0, 0))],
        out_specs=[blk, blk, blk],
        out_shape=[jax.ShapeDtypeStruct((m, QK_W), F32)] * 3,
        compiler_params=_cparams(("arbitrary",)),
        name="kvq_step",
    )(x, w_kvq)


def _lambda(lq1_ref, lk1_ref, lq2_ref, lk2_ref, lam_init):
    s1 = jnp.sum(lq1_ref[...] * lk1_ref[...], axis=-1, keepdims=True)
    s2 = jnp.sum(lq2_ref[...] * lk2_ref[...], axis=-1, keepdims=True)
    return jnp.exp(s1) - jnp.exp(s2) + lam_init


def _head_rms(o, g, lam_init):
    return o * lax.rsqrt(jnp.mean(o * o, axis=-1, keepdims=True) + RMS_EPS) * g * (1.0 - lam_init)


def _attn_prompt_tile(qi, alongside, q_ref, k_ref, vt_ref, lq1_ref, lk1_ref, lq2_ref, lk2_ref, g_ref, o_ref,
                      qs_ref, s0_ref, s1_ref, m_ref, acc_ref, *, t, lam_init):
    q = q_ref[0].astype(F32)
    lane = lax.broadcasted_iota(jnp.int32, (t, HEAD_W), 1)
    qs_ref[...] = jnp.concatenate([jnp.where(lane < HEAD_DIM_B, q, 0.0),
                                   jnp.where(lane >= HEAD_DIM_B, q, 0.0)], axis=0).astype(BF16)
    m_ref[...] = jnp.full_like(m_ref, -jnp.inf)
    acc_ref[...] = jnp.zeros_like(acc_ref)

    def scores(ki, s_ref):
        start = pl.multiple_of(ki * t, t)
        k = k_ref[0, pl.ds(start, t), :]
        s_ref[...] = lax.dot_general(k, qs_ref[...], (((1,), (1,)), ((), ())),
                                     preferred_element_type=F32)

    def update(ki, s_ref, masked):
        s = s_ref[...]
        if masked:
            key = lax.broadcasted_iota(jnp.int32, (t, 2 * t), 0)
            qry = lax.broadcasted_iota(jnp.int32, (t, 2 * t), 1) % t
            s = jnp.where(key <= qry, s, -jnp.inf)
        m_old = m_ref[...]
        m_new = jnp.maximum(m_old, jnp.max(s, axis=0, keepdims=True))
        p = jnp.exp2(s - m_new).astype(BF16)
        vt = vt_ref[0, 0, ki]
        acc_ref[...] = jnp.exp2(m_old - m_new) * acc_ref[...] + jnp.dot(vt, p, preferred_element_type=F32)
        m_ref[...] = m_new

    scores(0, s0_ref)
    alongside()

    def pair(p, carry):
        scores(2 * p + 1, s1_ref)
        update(2 * p, s0_ref, False)
        scores(2 * p + 2, s0_ref)
        update(2 * p + 1, s1_ref, False)
        return carry

    lax.fori_loop(0, qi // 2, pair, 0)

    @pl.when(qi % 2 == 0)
    def _():
        update(qi, s0_ref, True)

    @pl.when(qi % 2 == 1)
    def _():
        scores(qi, s1_ref)
        update(qi - 1, s0_ref, False)
        update(qi, s1_ref, True)

    lam = _lambda(lq1_ref, lk1_ref, lq2_ref, lk2_ref, lam_init)
    acc = acc_ref[:HEAD_W, :]
    l = acc_ref[HEAD_W:HEAD_W + 1, :]
    o = acc[:, :t] / l[:, :t] - lam * (acc[:, t:] / l[:, t:])
    o = o * lax.rsqrt(jnp.mean(o * o, axis=0, keepdims=True) + RMS_EPS) * g_ref[...] * (1.0 - lam_init)
    o_ref[0] = o.T.astype(o_ref.dtype)


def _attn_step_part(phase, seq, j, is_last, q_ref, kn_ref, vn_ref, lq1_ref, lk1_ref, lq2_ref, lk2_ref, g_ref,
                    k_refs, v_refs, o_ref, qcol_ref, qblk_ref, m_ref, l_ref, acc_ref, *, pages, page, lam_init):
    nmap = 2 * N_HEADS_B
    row16 = lax.broadcasted_iota(jnp.int32, (nmap, page), 0)
    r = seq % 8
    if phase == "pages":
        _attn_step_pages(row16, k_refs, v_refs, qcol_ref, m_ref, l_ref, acc_ref, pages=pages, page=page)
        return
    if phase == "last":
        _attn_step_last(r, is_last, kn_ref, vn_ref, lq1_ref, lk1_ref, lq2_ref, lk2_ref, g_ref, o_ref,
                        qblk_ref, m_ref, l_ref, acc_ref, lam_init=lam_init)
        return

    @pl.when(j == 0)
    def _():
        rq = lax.broadcasted_iota(jnp.int32, (nmap, QK_W), 0)
        cq = lax.broadcasted_iota(jnp.int32, (nmap, QK_W), 1)
        qrow = q_ref[pl.ds(r, 1), :]
        qblk_ref[...] = jnp.where(cq // HEAD_DIM_B == rq, jnp.broadcast_to(qrow, (nmap, QK_W)), 0.0)
        qcol_ref[...] = jnp.broadcast_to(qrow, (LANES, QK_W)).T
        m_ref[...] = jnp.full((nmap, 1), -jnp.inf, F32)
        l_ref[...] = jnp.zeros((nmap, 1), F32)
        acc_ref[...] = jnp.zeros((nmap, HEAD_W), F32)


def _attn_step_pages(row16, k_refs, v_refs, qcol_ref, m_ref, l_ref, acc_ref, *, pages, page):
    nmap = 2 * N_HEADS_B
    s_pages = [jnp.zeros((nmap, page), F32) for _ in range(pages)]
    for c in range(nmap):
        cs = slice(c * HEAD_DIM_B, (c + 1) * HEAD_DIM_B)
        q_c = qcol_ref[cs, :]
        for i in range(pages):
            blk = jnp.sum(k_refs[i][0, cs, :] * q_c, axis=0, keepdims=True)
            s_pages[i] = jnp.where(row16 == c, jnp.broadcast_to(blk, (nmap, page)), s_pages[i])
    s = jnp.concatenate(s_pages, axis=1)
    m_old = m_ref[...]
    m_new = jnp.maximum(m_old, jnp.max(s, axis=-1, keepdims=True))
    alpha = jnp.exp(m_old - m_new)
    p = jnp.exp(s - m_new)
    l_ref[...] = alpha * l_ref[...] + jnp.sum(p, axis=-1, keepdims=True)
    m_ref[...] = m_new
    pv = jnp.zeros((nmap, HEAD_W), F32)
    for i in range(pages):
        p_i = p[:, i * page:(i + 1) * page]
        for h in range(N_HEADS_B):
            v_h = v_refs[i][0, pl.ds(h, page, stride=N_HEADS_B), :]
            p_h = jnp.where(row16 // 2 == h, p_i, 0.0).astype(BF16)
            pv = pv + jnp.dot(p_h, v_h.astype(BF16), preferred_element_type=F32)
    acc_ref[...] = alpha * acc_ref[...] + pv


def _attn_step_last(r, is_last, kn_ref, vn_ref, lq1_ref, lk1_ref, lq2_ref, lk2_ref, g_ref, o_ref,
                    qblk_ref, m_ref, l_ref, acc_ref, *, lam_init):
    nmap = 2 * N_HEADS_B

    @pl.when(is_last)
    def _():
        s_new = jnp.sum(qblk_ref[...] * kn_ref[pl.ds(r, 1), :], axis=-1, keepdims=True)
        m_prev = m_ref[...]
        m_fin = jnp.maximum(m_prev, s_new)
        a_fin = jnp.exp(m_prev - m_fin)
        p_new = jnp.exp(s_new - m_fin)
        l = a_fin * l_ref[...] + p_new
        vn = vn_ref[pl.ds(r, 1), :]
        rowh = lax.broadcasted_iota(jnp.int32, (nmap, HEAD_W), 0)
        vn_rows = jnp.zeros((nmap, HEAD_W), F32)
        for h in range(N_HEADS_B):
            vn_rows = jnp.where(rowh // 2 == h,
                                jnp.broadcast_to(vn[:, h * HEAD_W:(h + 1) * HEAD_W], (nmap, HEAD_W)), vn_rows)
        acc = a_fin * acc_ref[...] + p_new * vn_rows
        lam = _lambda(lq1_ref, lk1_ref, lq2_ref, lk2_ref, lam_init)
        scaled = acc * (jnp.where(rowh % 2 == 0, 1.0, -lam) / l)
        outs = [_head_rms(scaled[2 * h:2 * h + 1, :] + scaled[2 * h + 1:2 * h + 2, :], g_ref[...], lam_init)
                for h in range(N_HEADS_B)]
        o_ref[pl.ds(r, 1), :] = jnp.concatenate(outs, axis=1)


N_PROMPT_IN = 8
N_SAMPLE_IN = 4


def _attn_both_kernel(pt_ref, *refs, t, nq, parts, pages, page, lam_init):
    prompt_in = refs[:N_PROMPT_IN]
    sq_ref, kn_ref, vn_ref, grow_ref = refs[N_PROMPT_IN:N_PROMPT_IN + N_SAMPLE_IN]
    base = N_PROMPT_IN + N_SAMPLE_IN
    k_refs = refs[base:base + pages]
    v_refs = refs[base + pages:base + 2 * pages]
    o_ref, so_ref = refs[base + 2 * pages:base + 2 * pages + 2]
    (qs_ref, s0_ref, s1_ref, m_ref, acc_ref,
     qcol_ref, qblk_ref, sm_ref, sl_ref, sacc_ref) = refs[base + 2 * pages + 2:]
    qi = pl.program_id(2)
    step = (pl.program_id(0) * N_HEADS_B + pl.program_id(1)) * nq + qi
    lams = prompt_in[3:7]
    step_part = functools.partial(
        _attn_step_part, seq=step // parts, j=step % parts, is_last=step % parts == parts - 1,
        q_ref=sq_ref, kn_ref=kn_ref, vn_ref=vn_ref, lq1_ref=lams[0], lk1_ref=lams[1], lq2_ref=lams[2], lk2_ref=lams[3],
        g_ref=grow_ref, k_refs=k_refs, v_refs=v_refs, o_ref=so_ref, qcol_ref=qcol_ref, qblk_ref=qblk_ref,
        m_ref=sm_ref, l_ref=sl_ref, acc_ref=sacc_ref, pages=pages, page=page, lam_init=lam_init)
    step_part("first")
    _attn_prompt_tile(qi, functools.partial(step_part, "pages"), *prompt_in, o_ref, qs_ref, s0_ref, s1_ref, m_ref,
                      acc_ref, t=t, lam_init=lam_init)
    step_part("last")


def _attn_both(qb, kb, vtb, sq, k_new, v_new, cache_kt, cache_vf, page_table, lq1, lk1, lq2, lk2, g, lam_init, t):
    bsz, seq, _ = qb.shape
    n = sq.shape[0]
    n_pages = page_table.shape[1]
    page = cache_kt.shape[2]
    nq = seq // t
    steps = bsz * N_HEADS_B * nq
    assert steps % n == 0, "every sample sequence gets the same number of grid steps"
    parts = steps // n
    assert n_pages % parts == 0
    pages = n_pages // parts
    nmap = 2 * N_HEADS_B

    def step_of(b, h, i):
        return (b * N_HEADS_B + h) * nq + i

    vec = pl.BlockSpec((1, HEAD_DIM_B), lambda b, h, i, pt: (0, 0))
    rows = pl.BlockSpec((8, QK_W), lambda b, h, i, pt: (step_of(b, h, i) // parts // 8, 0))

    def page_spec(shape, slot):
        def index(b, h, i, pt):
            s = step_of(b, h, i)
            return (pt[(s // parts) * n_pages + (s % parts) * pages + slot], 0, 0)
        return pl.BlockSpec((1,) + shape, index)

    grid_spec = pltpu.PrefetchScalarGridSpec(
        num_scalar_prefetch=1,
        grid=(bsz, N_HEADS_B, nq),
        in_specs=[pl.BlockSpec((1, t, HEAD_W), lambda b, h, i, pt: (b, i, h)),
                  pl.BlockSpec((1, seq, HEAD_W), lambda b, h, i, pt: (b, 0, h)),
                  pl.BlockSpec((1, 1, nq, VT_ROWS, t), lambda b, h, i, pt: (b, h, 0, 0, 0)),
                  vec, vec, vec, vec,
                  pl.BlockSpec((HEAD_W, 1), lambda b, h, i, pt: (0, 0)),
                  rows, rows, rows,
                  pl.BlockSpec((1, HEAD_W), lambda b, h, i, pt: (0, 0))]
                 + [page_spec((QK_W, page), slot) for slot in range(pages)]
                 + [page_spec((page * N_HEADS_B, HEAD_W), slot) for slot in range(pages)],
        out_specs=[pl.BlockSpec((1, t, HEAD_W), lambda b, h, i, pt: (b, i, h)), rows],
        scratch_shapes=[pltpu.VMEM((2 * t, HEAD_W), BF16), pltpu.VMEM((t, 2 * t), F32), pltpu.VMEM((t, 2 * t), F32),
                        pltpu.VMEM((1, 2 * t), F32), pltpu.VMEM((VT_ROWS, 2 * t), F32),
                        pltpu.VMEM((QK_W, LANES), F32), pltpu.VMEM((nmap, QK_W), F32),
                        pltpu.VMEM((nmap, 1), F32), pltpu.VMEM((nmap, 1), F32), pltpu.VMEM((nmap, HEAD_W), F32)],
    )
    return pl.pallas_call(
        functools.partial(_attn_both_kernel, t=t, nq=nq, parts=parts, pages=pages, page=page, lam_init=lam_init),
        grid_spec=grid_spec,
        out_shape=[jax.ShapeDtypeStruct((bsz, seq, QK_W), BF16), jax.ShapeDtypeStruct((n, QK_W), F32)],
        compiler_params=_cparams(("arbitrary", "arbitrary", "arbitrary")),
        name="attn_both",
    )(page_table.reshape(-1), qb, kb, vtb, lq1, lk1, lq2, lk2, g.reshape(HEAD_W, 1),
      sq, k_new, v_new, g.reshape(1, HEAD_W), *([cache_kt] * pages), *([cache_vf] * pages))


def _lambda_init(layer):
    return 0.8 - 0.6 * math.exp(-0.3 * layer)


def _row_tile(m, want):
    return want if m % want == 0 else m


def _in_proj(x, prm):
    tm = _row_tile(x.shape[0], 1024)
    return _proj(x, prm["w_in"], tm, D_INNER), _proj(x, prm["w_dt"], tm, LANES)


def _layer0_tail(x, y, prm):
    m = x.shape[0]
    x = _proj_ln(y, prm["w_out"], x, prm["ln_g"][0, 0], prm["ln_b"][0, 0], _row_tile(m, 512))
    return _mlp(x, prm["w_up"][0], prm["w_down"][0], prm["ln_g"][0, 1], prm["ln_b"][0, 1], _row_tile(m, 1024), 1024)


def _layer1_tail(x, o, prm):
    m = x.shape[0]
    x = _proj_ln(o, prm["w_o"], x, prm["ln_g"][1, 0], prm["ln_b"][1, 0], _row_tile(m, 512))
    return _mlp(x, prm["w_up"][1], prm["w_down"][1], prm["ln_g"][1, 1], prm["ln_b"][1, 1], _row_tile(m, 1024), 1024)


def kernel(x_prompt, x_sample, state_conv, state_ssm, cache_k, cache_v, page_table, w_in_a, conv_w_a, conv_b_a, dt_bias_a, a_log_a, d_skip_a, norm_a, w_out_a, w_kv, w_q_b, lambda_q1_b, lambda_k1_b, lambda_q2_b, lambda_k2_b, subln_b, w_o_b, w_up, w_down, ln_g, ln_b):
    bp, sp, _ = x_prompt.shape
    bd, ts, _ = x_sample.shape
    assert ts == 1 and sp % ATTN_TILE == 0 and bd % 8 == 0

    w_in = w_in_a[0]
    split = D_INNER + CONV_DIM
    prm = {
        "w_in": w_in[:, :split].astype(BF16),
        "w_dt": jnp.tile(w_in[:, split:], (1, DT_REP)).astype(BF16),
        "w_out": w_out_a[0].astype(BF16),
        "w_kvq": jnp.concatenate([w_kv, w_q_b[0]], axis=1).astype(BF16),
        "w_o": w_o_b[0].astype(BF16),
        "w_up": w_up.astype(BF16),
        "w_down": w_down.astype(BF16),
        "ln_g": ln_g.reshape(DEPTH, 2, 1, D_MODEL),
        "ln_b": ln_b.reshape(DEPTH, 2, 1, D_MODEL),
    }
    cw = conv_w_a[0]
    cb = conv_b_a[0].reshape(1, CONV_DIM)
    dtb = jnp.tile(dt_bias_a[0], DT_REP).reshape(1, LANES)
    alog = jnp.tile(a_log_a[0], DT_REP).reshape(1, LANES)
    dskip = jnp.repeat(d_skip_a[0], HEADDIM_A).reshape(1, D_INNER)
    ng = norm_a[0].reshape(1, D_INNER)
    r128 = _rep_matrix(SSD_CHUNK)
    r64 = _rep_matrix(HEADDIM_A)
    lam_init = _lambda_init(DEPTH // 2)
    lvec = [a[0].reshape(1, HEAD_DIM_B) for a in (lambda_q1_b, lambda_k1_b, lambda_q2_b, lambda_k2_b)]

    xp0 = x_prompt.reshape(bp * sp, D_MODEL)
    xs0 = x_sample.reshape(bd, D_MODEL)
    zx_p, dtr_p = _in_proj(xp0, prm)
    zx_s, dtr_s = _in_proj(xs0, prm)
    zx_p = zx_p.reshape(bp, sp, ZX_W)
    ya_p, ssm_p, ya_s, cst_s, ssm_s = _ssd_both(zx_p, dtr_p.reshape(bp, sp, LANES), zx_s, dtr_s,
                                                state_conv[0].transpose(1, 0, 2), state_ssm[0],
                                                cw, cb, dtb, alog, dskip, ng, r128, r64)
    conv_p = zx_p[:, sp - (CONV_W - 1):, D_INNER:]
    conv_s = cst_s.transpose(1, 0, 2)
    x_p = _layer0_tail(xp0, ya_p.reshape(bp * sp, D_INNER), prm)
    x_s = _layer0_tail(xs0, ya_s, prm)

    kt_p, v_p, kb, vtb, qb = _kvq_prompt(x_p, prm["w_kvq"], bp, sp, ATTN_TILE)
    k_s, v_s, q_s = _kvq_step(x_s, prm["w_kvq"], PAST_LEN)

    n_phys, page = cache_k.shape[:2]
    cache_kt = cache_k.transpose(0, 2, 3, 4, 1).reshape(n_phys, QK_W, page)
    cache_vf = cache_v.reshape(n_phys, page * N_HEADS_B, HEAD_W)
    shp = (bp, sp, QK_W)
    o_p, o_s = _attn_both(qb.reshape(shp), kb.reshape(shp), vtb, q_s, k_s, v_s, cache_kt, cache_vf, page_table,
                          *lvec, subln_b[0], lam_init, ATTN_TILE)

    y_p = _layer1_tail(x_p, o_p.reshape(bp * sp, QK_W), prm)
    y_s = _layer1_tail(x_s, o_s, prm)

    k_p = kt_p.reshape(bp, N_HEADS_B, 2, HEAD_DIM_B, sp).transpose(0, 4, 1, 2, 3)
    return (y_p.reshape(bp, sp, D_MODEL), y_s.reshape(bd, 1, D_MODEL),
            conv_p[None], ssm_p[None], k_p, v_p.reshape(bp, sp, N_HEADS_B, HEAD_W),
            conv_s[None], ssm_s[None],
            k_s.reshape(bd, 1, N_HEADS_B, 2, HEAD_DIM_B), v_s.reshape(bd, 1, N_HEADS_B, HEAD_W))
```

```python
import functools
import math

import jax
import jax.numpy as jnp
from jax import lax
from jax.experimental import pallas as pl
from jax.experimental.pallas import tpu as pltpu

F32 = jnp.float32
BF16 = jnp.bfloat16

D_MODEL = 1024
DEPTH = 2
PAST_LEN = 2048

D_INNER = 2048
HEADDIM_A = 64
N_HEADS_A = 32
D_STATE = 128
N_GROUPS_A = 8
HEADS_PER_GROUP = N_HEADS_A // N_GROUPS_A
GROUP_W = HEADS_PER_GROUP * HEADDIM_A
CONV_W = 4
CONV_DIM = D_INNER + 2 * N_GROUPS_A * D_STATE
SSD_CHUNK = 128
DT_REP = 4
ZX_W = D_INNER + CONV_DIM

HEAD_DIM_B = 64
N_HEADS_B = 8
HEAD_W = 2 * HEAD_DIM_B
QK_W = N_HEADS_B * HEAD_W
ROPE_THETA = 10000.0
ATTN_TILE = 512
VT_ROWS = HEAD_W + 16
Q_SCALE_LOG2 = HEAD_DIM_B ** -0.5 * math.log2(math.e)

D_FF = 4 * D_MODEL
DN_ALPHA = (2 * DEPTH) ** 0.25
LN_EPS = 1e-5
RMS_EPS = 1e-5

LANES = 128
SUBLANES = 8
VMEM_LIMIT = 56 * 1024 * 1024


def _cparams(sem):
    return pltpu.CompilerParams(dimension_semantics=sem, vmem_limit_bytes=VMEM_LIMIT)


def _layer_norm(r, g, b):
    mu = jnp.mean(r, axis=-1, keepdims=True)
    d = r - mu
    var = jnp.mean(d * d, axis=-1, keepdims=True)
    return d * lax.rsqrt(var + LN_EPS) * g + b


def _silu(x):
    h = 0.5 * x
    return h + h * jnp.tanh(h)


def _softplus(x):
    return jnp.maximum(x, 0.0) + jnp.log1p(jnp.exp(-jnp.abs(x)))


def _split3(v):
    hi = v.astype(BF16).astype(F32)
    r1 = v - hi
    mid = r1.astype(BF16).astype(F32)
    lo = r1 - mid
    lane = lax.broadcasted_iota(jnp.int32, v.shape, 1)
    return jnp.where(lane < 32, hi, jnp.where(lane < 64, mid, jnp.where(lane < 96, lo, 0.0))).astype(BF16)


def _rep_matrix(width):
    r = jnp.arange(LANES)[:, None]
    c = jnp.arange(N_HEADS_A * width)[None, :]
    return ((r < 96) & ((r % 32) == (c // width))).astype(BF16)


def _proj_kernel(x_ref, w_ref, o_ref, xb_ref):
    @pl.when(pl.program_id(1) == 0)
    def _():
        xb_ref[...] = x_ref[...].astype(BF16)

    o_ref[...] = jnp.dot(xb_ref[...], w_ref[...], preferred_element_type=F32).astype(o_ref.dtype)


def _proj(x, w, tm, tn):
    m, k = x.shape
    n = w.shape[1]
    return pl.pallas_call(
        _proj_kernel,
        grid=(m // tm, n // tn),
        in_specs=[pl.BlockSpec((tm, k), lambda i, j: (i, 0)),
                  pl.BlockSpec((k, tn), lambda i, j: (0, j))],
        out_specs=pl.BlockSpec((tm, tn), lambda i, j: (i, j)),
        out_shape=jax.ShapeDtypeStruct((m, n), F32),
        scratch_shapes=[pltpu.VMEM((tm, k), BF16)],
        compiler_params=_cparams(("parallel", "arbitrary")),
        name="in_proj",
    )(x, w)


def _proj_ln_kernel(x_ref, w_ref, r_ref, g_ref, b_ref, o_ref):
    y = jnp.dot(x_ref[...].astype(BF16), w_ref[...], preferred_element_type=F32)
    o_ref[...] = _layer_norm(DN_ALPHA * r_ref[...] + y, g_ref[...], b_ref[...])


def _proj_ln(x, w, resid, g, b, tm):
    m, k = x.shape
    n = w.shape[1]
    return pl.pallas_call(
        _proj_ln_kernel,
        grid=(m // tm,),
        in_specs=[pl.BlockSpec((tm, k), lambda i: (i, 0)),
                  pl.BlockSpec((k, n), lambda i: (0, 0)),
                  pl.BlockSpec((tm, n), lambda i: (i, 0)),
                  pl.BlockSpec((1, n), lambda i: (0, 0)),
                  pl.BlockSpec((1, n), lambda i: (0, 0))],
        out_specs=pl.BlockSpec((tm, n), lambda i: (i, 0)),
        out_shape=jax.ShapeDtypeStruct((m, n), F32),
        compiler_params=_cparams(("parallel",)),
        name="proj_ln",
    )(x, w, resid, g, b)


def _mlp_kernel(x_ref, wu_ref, wd_ref, g_ref, b_ref, o_ref, xb_ref, acc_ref):
    j = pl.program_id(1)

    @pl.when(j == 0)
    def _():
        xb_ref[...] = x_ref[...].astype(BF16)
        acc_ref[...] = jnp.zeros_like(acc_ref)

    h = jnp.dot(xb_ref[...], wu_ref[...], preferred_element_type=F32)
    h = jnp.square(jnp.maximum(h, 0.0)).astype(BF16)
    acc_ref[...] += jnp.dot(h, wd_ref[...], preferred_element_type=F32)

    @pl.when(j == pl.num_programs(1) - 1)
    def _():
        o_ref[...] = _layer_norm(DN_ALPHA * x_ref[...] + acc_ref[...], g_ref[...], b_ref[...])


def _mlp(x, w_up, w_down, g, b, tm, tf):
    m, d = x.shape
    ff = w_up.shape[1]
    return pl.pallas_call(
        _mlp_kernel,
        grid=(m // tm, ff // tf),
        in_specs=[pl.BlockSpec((tm, d), lambda i, j: (i, 0)),
                  pl.BlockSpec((d, tf), lambda i, j: (0, j)),
                  pl.BlockSpec((tf, d), lambda i, j: (j, 0)),
                  pl.BlockSpec((1, d), lambda i, j: (0, 0)),
                  pl.BlockSpec((1, d), lambda i, j: (0, 0))],
        out_specs=pl.BlockSpec((tm, d), lambda i, j: (i, 0)),
        out_shape=jax.ShapeDtypeStruct((m, d), F32),
        scratch_shapes=[pltpu.VMEM((tm, d), BF16), pltpu.VMEM((tm, d), F32)],
        compiler_params=_cparams(("parallel", "arbitrary")),
        name="mlp_ln",
    )(x, w_up, w_down, g, b)


def _shift_rows(cur, prev8, k):
    rolled = pltpu.roll(cur, k, 0)
    rolled_prev = pltpu.roll(prev8, k, 0)
    row = lax.broadcasted_iota(jnp.int32, prev8.shape, 0)
    top = jnp.where(row < k, rolled_prev, rolled[:8])
    return jnp.concatenate([top, rolled[8:]], axis=0)


def _ssd_chunk(c, is_last, alongside, z_ref, x_ref, bc_ref, dt_ref, cw_ref, cb_ref, dtb_ref, alog_ref, dskip_ref,
               ng_ref, r128_ref, r64_ref, y_ref, st_ref, halo_ref, state_ref):
    q = SSD_CHUNK

    @pl.when(c == 0)
    def _():
        halo_ref[...] = jnp.zeros_like(halo_ref)
        state_ref[...] = jnp.zeros_like(state_ref)

    raw = jnp.concatenate([x_ref[0], bc_ref[0]], axis=1)
    prev8 = halo_ref[...]
    conv = cb_ref[...] + cw_ref[CONV_W - 1:CONV_W, :] * raw
    for k in range(1, CONV_W):
        conv = conv + cw_ref[CONV_W - 1 - k:CONV_W - k, :] * _shift_rows(raw, prev8, k)
    halo_ref[...] = raw[q - 8:, :]
    conv = _silu(conv)
    xc = conv[:, :D_INNER]
    bmat = conv[:, D_INNER:D_INNER + N_GROUPS_A * D_STATE]
    cmat = conv[:, D_INNER + N_GROUPS_A * D_STATE:]

    dt = _softplus(dt_ref[0] + dtb_ref[...])
    da = dt * (-jnp.exp(alog_ref[...]))
    ri = lax.broadcasted_iota(jnp.int32, (q, q), 0)
    ci = lax.broadcasted_iota(jnp.int32, (q, q), 1)
    causal = ri >= ci
    tri = causal.astype(BF16)
    hi = da.astype(BF16)
    r1 = da - hi.astype(F32)
    mid = r1.astype(BF16)
    lo = (r1 - mid.astype(F32)).astype(BF16)
    cs = (jnp.dot(tri, hi, preferred_element_type=F32)
          + jnp.dot(tri, mid, preferred_element_type=F32)
          + jnp.dot(tri, lo, preferred_element_type=F32))
    cs_t = cs.T
    cs_last = cs[q - 1:q, :]

    r64 = r64_ref[...]
    dt_rep = jnp.dot(_split3(dt), r64, preferred_element_type=F32)
    e_rep = jnp.dot(_split3(jnp.exp(cs)), r64, preferred_element_type=F32)
    dte_rep = jnp.dot(_split3(jnp.exp(cs_last - cs)), r64, preferred_element_type=F32)
    cs_rep = jnp.dot(_split3(cs), r128_ref[...], preferred_element_type=F32)

    xs = xc * dt_rep
    xs_b = xs.astype(BF16)
    xe_b = (xs * dte_rep).astype(BF16)
    e_last = e_rep[q - 1:q, :]

    y_groups = []
    for g in range(N_GROUPS_A):
        b_g = bmat[:, g * D_STATE:(g + 1) * D_STATE]
        c_g = cmat[:, g * D_STATE:(g + 1) * D_STATE].astype(BF16)
        b_gt = b_g.T.astype(BF16)
        cb = jnp.dot(c_g, b_gt, preferred_element_type=F32)
        gs = slice(g * GROUP_W, (g + 1) * GROUP_W)
        st_g = state_ref[g]
        y_off = jnp.dot(c_g, st_g.astype(BF16), preferred_element_type=F32) * e_rep[:, gs]
        y_heads = []
        for r in range(HEADS_PER_GROUP):
            h = g * HEADS_PER_GROUP + r
            diff = cs_rep[:, h * q:(h + 1) * q] - cs_t[h:h + 1, :]
            w = (cb * jnp.exp(jnp.where(causal, diff, -jnp.inf))).astype(BF16)
            y_heads.append(jnp.dot(w, xs_b[:, h * HEADDIM_A:(h + 1) * HEADDIM_A],
                                   preferred_element_type=F32))
        y_groups.append(jnp.concatenate(y_heads, axis=1) + y_off)
        state_ref[g] = st_g * e_last[:, gs] + jnp.dot(b_gt, xe_b[:, gs], preferred_element_type=F32)

    y = jnp.concatenate(y_groups, axis=1) + dskip_ref[...] * xc
    y = y * _silu(z_ref[0])
    outs = []
    for g in range(N_GROUPS_A):
        yg = y[:, g * GROUP_W:(g + 1) * GROUP_W]
        outs.append(yg * lax.rsqrt(jnp.mean(yg * yg, axis=-1, keepdims=True) + RMS_EPS))
    y_ref[0] = (jnp.concatenate(outs, axis=1) * ng_ref[...]).astype(y_ref.dtype)
    alongside()

    @pl.when(is_last)
    def _():
        for g in range(N_GROUPS_A):
            st_ref[0, g] = state_ref[g].T


def _ssd_step_pre_kernel(dtr_ref, dtb_ref, alog_ref, r64_ref, r128_ref, dtrep_ref, dabc_ref):
    dt = _softplus(dtr_ref[...] + dtb_ref[...])
    da = jnp.exp(dt * (-jnp.exp(alog_ref[...])))
    dtrep_ref[...] = jnp.dot(_split3(dt), r64_ref[...], preferred_element_type=F32)
    dabc_ref[...] = jnp.dot(_split3(da), r128_ref[...], preferred_element_type=F32)


def _ssd_step_pre(dtr, dtb, alog, r64, r128):
    n = dtr.shape[0]
    full = lambda shape: pl.BlockSpec(shape, lambda i: (0,) * len(shape))
    return pl.pallas_call(
        _ssd_step_pre_kernel,
        grid=(1,),
        in_specs=[full((n, LANES)), full((1, LANES)), full((1, LANES)),
                  full((LANES, D_INNER)), full((LANES, N_HEADS_A * LANES))],
        out_specs=[full((n, D_INNER)), full((n, N_HEADS_A * LANES))],
        out_shape=[jax.ShapeDtypeStruct((n, D_INNER), F32), jax.ShapeDtypeStruct((n, N_HEADS_A * LANES), F32)],
        compiler_params=_cparams(("arbitrary",)),
        name="ssd_step_pre",
    )(dtr, dtb, alog, r64, r128)


def _ssd_step_seq(seq, zx_ref, dtrep_ref, dabc_ref, cst_ref, h_ref, cw_ref, cb_ref, dskip_ref, ng_ref,
                  y_ref, cst_out_ref, h_out_ref):
    r = seq % SUBLANES
    zx = zx_ref[pl.ds(r, 1), :]
    raw = zx[:, D_INNER:D_INNER + CONV_DIM]
    conv = cb_ref[...] + cw_ref[CONV_W - 1:CONV_W, :] * raw
    for k in range(CONV_W - 1):
        prev = cst_ref[k, pl.ds(r, 1), :]
        conv = conv + cw_ref[k:k + 1, :] * prev
        if k > 0:
            cst_out_ref[k - 1, pl.ds(r, 1), :] = prev
    cst_out_ref[CONV_W - 2, pl.ds(r, 1), :] = raw
    conv = _silu(conv)
    xc = conv[:, :D_INNER]
    bmat = conv[:, D_INNER:D_INNER + N_GROUPS_A * D_STATE]
    cmat = conv[:, D_INNER + N_GROUPS_A * D_STATE:]

    xdt = xc * dtrep_ref[pl.ds(r, 1), :]
    da = dabc_ref[pl.ds(r, 1), :]

    xdt_col = jnp.broadcast_to(xdt, (LANES, D_INNER)).T

    row8 = lax.broadcasted_iota(jnp.int32, (8, D_STATE), 0)
    c_rows = jnp.zeros((8, D_STATE), F32)
    for g in range(N_GROUPS_A):
        c_rows = jnp.where(row8 == g, jnp.broadcast_to(cmat[:, g * D_STATE:(g + 1) * D_STATE], (8, D_STATE)), c_rows)
    c_rows = c_rows.astype(BF16)

    y_rows = []
    for g in range(N_GROUPS_A):
        gs = slice(g * GROUP_W, (g + 1) * GROUP_W)
        b_g = bmat[:, g * D_STATE:(g + 1) * D_STATE]
        heads = []
        for hh in range(g * HEADS_PER_GROUP, (g + 1) * HEADS_PER_GROUP):
            hs = slice(hh * HEADDIM_A, (hh + 1) * HEADDIM_A)
            heads.append(h_ref[0, hs, :] * da[:, hh * LANES:(hh + 1) * LANES] + xdt_col[hs, :] * b_g)
        h_new = jnp.concatenate(heads, axis=0)
        h_out_ref[0, gs, :] = h_new
        yg = lax.dot_general(c_rows, h_new.astype(BF16), (((1,), (1,)), ((), ())),
                             preferred_element_type=F32)
        y_rows.append(yg[g:g + 1, :])
    y = jnp.concatenate(y_rows, axis=1) + dskip_ref[...] * xc
    y = y * _silu(zx[:, :D_INNER])
    outs = []
    for g in range(N_GROUPS_A):
        yg = y[:, g * GROUP_W:(g + 1) * GROUP_W]
        outs.append(yg * lax.rsqrt(jnp.mean(yg * yg, axis=-1, keepdims=True) + RMS_EPS))
    y_ref[pl.ds(r, 1), :] = jnp.concatenate(outs, axis=1) * ng_ref[...]


N_CHUNK_IN = 12
N_SEQ_IN = 5


def _ssd_both_kernel(*refs, nc):
    chunk_in = refs[:N_CHUNK_IN]
    seq_in = refs[N_CHUNK_IN:N_CHUNK_IN + N_SEQ_IN]
    outs = refs[N_CHUNK_IN + N_SEQ_IN:N_CHUNK_IN + N_SEQ_IN + 5]
    halo_ref, state_ref = refs[N_CHUNK_IN + N_SEQ_IN + 5:]
    c = pl.program_id(1)
    cw_ref, cb_ref, dskip_ref, ng_ref = chunk_in[4], chunk_in[5], chunk_in[8], chunk_in[9]
    step_seq = functools.partial(_ssd_step_seq, pl.program_id(0) * nc + c, *seq_in, cw_ref, cb_ref, dskip_ref, ng_ref,
                                 *outs[2:])
    _ssd_chunk(c, c == nc - 1, step_seq, *chunk_in, outs[0], outs[1], halo_ref, state_ref)


def _ssd_both(zx, dtr, zx_s, dtr_s, conv_state_t, ssm_state, cw, cb, dtb, alog, dskip, ng, r128, r64):
    bsz, seq, _ = zx.shape
    n = zx_s.shape[0]
    q = SSD_CHUNK
    nc = seq // q
    assert bsz * nc == n, "one sample sequence per prompt chunk step"
    dt_rep, da_bc = _ssd_step_pre(dtr_s, dtb, alog, r64, r128)
    const = lambda shape: pl.BlockSpec(shape, lambda b, c: (0,) * len(shape))
    rows = lambda w: pl.BlockSpec((SUBLANES, w), lambda b, c: ((b * nc + c) // SUBLANES, 0))
    cst_blk = pl.BlockSpec((CONV_W - 1, SUBLANES, CONV_DIM), lambda b, c: (0, (b * nc + c) // SUBLANES, 0))
    h_blk = pl.BlockSpec((1, D_INNER, D_STATE), lambda b, c: (b * nc + c, 0, 0))
    y, st, y_s, cst, h = pl.pallas_call(
        functools.partial(_ssd_both_kernel, nc=nc),
        grid=(bsz, nc),
        in_specs=[pl.BlockSpec((1, q, D_INNER), lambda b, c: (b, c, 0)),
                  pl.BlockSpec((1, q, D_INNER), lambda b, c: (b, c, 1)),
                  pl.BlockSpec((1, q, D_INNER), lambda b, c: (b, c, 2)),
                  pl.BlockSpec((1, q, LANES), lambda b, c: (b, c, 0)),
                  const((CONV_W, CONV_DIM)), const((1, CONV_DIM)), const((1, LANES)), const((1, LANES)),
                  const((1, D_INNER)), const((1, D_INNER)),
                  const((LANES, N_HEADS_A * q)), const((LANES, D_INNER)),
                  rows(ZX_W), rows(D_INNER), rows(N_HEADS_A * LANES), cst_blk, h_blk],
        out_specs=[pl.BlockSpec((1, q, D_INNER), lambda b, c: (b, c, 0)),
                   pl.BlockSpec((1, N_GROUPS_A, GROUP_W, D_STATE), lambda b, c: (b, 0, 0, 0)),
                   rows(D_INNER), cst_blk, h_blk],
        out_shape=[jax.ShapeDtypeStruct((bsz, seq, D_INNER), BF16),
                   jax.ShapeDtypeStruct((bsz, N_GROUPS_A, GROUP_W, D_STATE), F32),
                   jax.ShapeDtypeStruct((n, D_INNER), F32),
                   jax.ShapeDtypeStruct((CONV_W - 1, n, CONV_DIM), F32),
                   jax.ShapeDtypeStruct((n, D_INNER, D_STATE), F32)],
        scratch_shapes=[pltpu.VMEM((8, CONV_DIM), F32),
                        pltpu.VMEM((N_GROUPS_A, D_STATE, GROUP_W), F32)],
        compiler_params=_cparams(("arbitrary", "arbitrary")),
        name="ssd_both",
    )(zx, zx, zx, dtr, cw, cb, dtb, alog, dskip, ng, r128, r64,
      zx_s, dt_rep, da_bc, conv_state_t, ssm_state.reshape(n, D_INNER, D_STATE))
    return (y, st.reshape(bsz, N_HEADS_A, HEADDIM_A, D_STATE),
            y_s, cst, h.reshape(n, N_HEADS_A, HEADDIM_A, D_STATE))


def _rope_angles(pos_col):
    lane = lax.broadcasted_iota(jnp.int32, pos_col.shape, 1)
    half = HEAD_DIM_B // 2
    idx = (lane % half).astype(F32)
    inv = jnp.exp(idx * (-math.log(ROPE_THETA) / half))
    ang = pos_col * inv
    return jnp.cos(ang), jnp.sin(ang)


def _rope_signed(sin):
    lane = lax.broadcasted_iota(jnp.int32, sin.shape, 1)
    return jnp.where((lane % HEAD_DIM_B) < HEAD_DIM_B // 2, -sin, sin)


def _rope(x, cos, sin_signed):
    n = x.shape[1]
    reps = n // LANES
    cos_f = jnp.concatenate([cos] * reps, axis=1)
    sin_f = jnp.concatenate([sin_signed] * reps, axis=1)
    half = HEAD_DIM_B // 2
    first = (lax.broadcasted_iota(jnp.int32, x.shape, 1) % HEAD_DIM_B) < half
    rot = jnp.where(first, pltpu.roll(x, n - half, 1), pltpu.roll(x, half, 1))
    return x * cos_f + rot * sin_f


def _kvq_values(x_ref, w_ref, cos, sin_signed):
    xb = x_ref[...].astype(BF16)
    k = _rope(jnp.dot(xb, w_ref[:, :QK_W], preferred_element_type=F32), cos, sin_signed)
    v = jnp.dot(xb, w_ref[:, QK_W:2 * QK_W], preferred_element_type=F32)
    q = _rope(jnp.dot(xb, w_ref[:, 2 * QK_W:], preferred_element_type=F32), cos, sin_signed)
    return k, v, q


def _kvq_prompt_kernel(x_ref, w_ref, kt_ref, v_ref, kb_ref, vt_ref, qb_ref, cos_in_ref, sin_in_ref, *, tm, seq):
    i = pl.program_id(0)

    @pl.when(i == 0)
    def _():
        within = lax.broadcasted_iota(jnp.int32, (tm, LANES), 0).astype(F32)
        cos_in_ref[...], sin_in_ref[...] = _rope_angles(within)

    start = jnp.full((SUBLANES, LANES), (i * tm) % seq, jnp.int32).astype(F32)
    cos_st, sin_st = _rope_angles(start)
    cos_st, sin_st = cos_st[0:1], sin_st[0:1]
    cos = cos_st * cos_in_ref[...] - sin_st * sin_in_ref[...]
    sin = sin_st * cos_in_ref[...] + cos_st * sin_in_ref[...]
    k, v, q = _kvq_values(x_ref, w_ref, cos, _rope_signed(sin))
    kt_ref[0] = k.T
    kb_ref[...] = k.astype(BF16)
    v_ref[...] = v
    vt_ref[0, :, 0, :HEAD_W, :] = v.T.reshape(N_HEADS_B, HEAD_W, tm).astype(BF16)
    vt_ref[0, :, 0, HEAD_W:, :] = jnp.ones((N_HEADS_B, VT_ROWS - HEAD_W, tm), BF16)
    qb_ref[...] = (q * Q_SCALE_LOG2).astype(BF16)


def _kvq_prompt(x, w_kvq, bsz, seq, tm):
    m = x.shape[0]
    nblk = seq // tm
    blk = pl.BlockSpec((tm, QK_W), lambda i: (i, 0))
    return pl.pallas_call(
        functools.partial(_kvq_prompt_kernel, tm=tm, seq=seq),
        grid=(m // tm,),
        in_specs=[pl.BlockSpec((tm, D_MODEL), lambda i: (i, 0)),
                  pl.BlockSpec((D_MODEL, 3 * QK_W), lambda i: (0, 0))],
        out_specs=[pl.BlockSpec((1, QK_W, tm), lambda i: (i // nblk, 0, i % nblk)),
                   blk, blk,
                   pl.BlockSpec((1, N_HEADS_B, 1, VT_ROWS, tm), lambda i: (i // nblk, 0, i % nblk, 0, 0)),
                   blk],
        out_shape=[jax.ShapeDtypeStruct((bsz, QK_W, seq), F32), jax.ShapeDtypeStruct((m, QK_W), F32),
                   jax.ShapeDtypeStruct((m, QK_W), BF16),
                   jax.ShapeDtypeStruct((bsz, N_HEADS_B, nblk, VT_ROWS, tm), BF16),
                   jax.ShapeDtypeStruct((m, QK_W), BF16)],
        scratch_shapes=[pltpu.VMEM((tm, LANES), F32), pltpu.VMEM((tm, LANES), F32)],
        compiler_params=_cparams(("arbitrary",)),
        name="kvq_prompt",
    )(x, w_kvq)


def _kvq_step_kernel(x_ref, w_ref, k_ref, v_ref, q_ref, *, tm, pos0):
    cos, sin = _rope_angles(jnp.full((tm, LANES), pos0, F32))
    k, v, q = _kvq_values(x_ref, w_ref, cos, _rope_signed(sin))
    k_ref[...] = k
    v_ref[...] = v
    q_ref[...] = q * (HEAD_DIM_B ** -0.5)


def _kvq_step(x, w_kvq, pos0):
    m = x.shape[0]
    blk = pl.BlockSpec((m, QK_W), lambda i: (0, 0))
    return pl.pallas_call(
        functools.partial(_kvq_step_kernel, tm=m, pos0=pos0),
        grid=(1,),
        in_specs=[pl.BlockSpec((m, D_MODEL), lambda i: (0, 0)),
                  pl.BlockSpec((D_MODEL, 3 * QK_W), lambda i: (0, 0))],
        out_specs=[blk, blk, blk],
        out_shape=[jax.ShapeDtypeStruct((m, QK_W), F32)] * 3,
        compiler_params=_cparams(("arbitrary",)),
        name="kvq_step",
    )(x, w_kvq)


def _lambda(lq1_ref, lk1_ref, lq2_ref, lk2_ref, lam_init):
    s1 = jnp.sum(lq1_ref[...] * lk1_ref[...], axis=-1, keepdims=True)
    s2 = jnp.sum(lq2_ref[...] * lk2_ref[...], axis=-1, keepdims=True)
    return jnp.exp(s1) - jnp.exp(s2) + lam_init


def _head_rms(o, g, lam_init):
    return o * lax.rsqrt(jnp.mean(o * o, axis=-1, keepdims=True) + RMS_EPS) * g * (1.0 - lam_init)


def _attn_prompt_tile(qi, alongside, q_ref, k_ref, vt_ref, lq1_ref, lk1_ref, lq2_ref, lk2_ref, g_ref, o_ref,
                      qs_ref, s0_ref, s1_ref, m_ref, acc_ref, *, t, lam_init):
    q = q_ref[0].astype(F32)
    lane = lax.broadcasted_iota(jnp.int32, (t, HEAD_W), 1)
    qs_ref[...] = jnp.concatenate([jnp.where(lane < HEAD_DIM_B, q, 0.0),
                                   jnp.where(lane >= HEAD_DIM_B, q, 0.0)], axis=0).astype(BF16)
    m_ref[...] = jnp.full_like(m_ref, -jnp.inf)
    acc_ref[...] = jnp.zeros_like(acc_ref)

    def scores(ki, s_ref):
        start = pl.multiple_of(ki * t, t)
        k = k_ref[0, pl.ds(start, t), :]
        s_ref[...] = lax.dot_general(k, qs_ref[...], (((1,), (1,)), ((), ())),
                                     preferred_element_type=F32)

    def update(ki, s_ref, masked):
        s = s_ref[...]
        if masked:
            key = lax.broadcasted_iota(jnp.int32, (t, 2 * t), 0)
            qry = lax.broadcasted_iota(jnp.int32, (t, 2 * t), 1) % t
            s = jnp.where(key <= qry, s, -jnp.inf)
        m_old = m_ref[...]
        m_new = jnp.maximum(m_old, jnp.max(s, axis=0, keepdims=True))
        p = jnp.exp2(s - m_new).astype(BF16)
        vt = vt_ref[0, 0, ki]
        acc_ref[...] = jnp.exp2(m_old - m_new) * acc_ref[...] + jnp.dot(vt, p, preferred_element_type=F32)
        m_ref[...] = m_new

    scores(0, s0_ref)
    alongside()

    def pair(p, carry):
        scores(2 * p + 1, s1_ref)
        update(2 * p, s0_ref, False)
        scores(2 * p + 2, s0_ref)
        update(2 * p + 1, s1_ref, False)
        return carry

    lax.fori_loop(0, qi // 2, pair, 0)

    @pl.when(qi % 2 == 0)
    def _():
        update(qi, s0_ref, True)

    @pl.when(qi % 2 == 1)
    def _():
        scores(qi, s1_ref)
        update(qi - 1, s0_ref, False)
        update(qi, s1_ref, True)

    lam = _lambda(lq1_ref, lk1_ref, lq2_ref, lk2_ref, lam_init)
    acc = acc_ref[:HEAD_W, :]
    l = acc_ref[HEAD_W:HEAD_W + 1, :]
    o = acc[:, :t] / l[:, :t] - lam * (acc[:, t:] / l[:, t:])
    o = o * lax.rsqrt(jnp.mean(o * o, axis=0, keepdims=True) + RMS_EPS) * g_ref[...] * (1.0 - lam_init)
    o_ref[0] = o.T.astype(o_ref.dtype)


def _attn_step_part(phase, seq, j, is_last, q_ref, kn_ref, vn_ref, lq1_ref, lk1_ref, lq2_ref, lk2_ref, g_ref,
                    k_refs, v_refs, o_ref, qcol_ref, qblk_ref, m_ref, l_ref, acc_ref, *, pages, page, lam_init):
    nmap = 2 * N_HEADS_B
    row16 = lax.broadcasted_iota(jnp.int32, (nmap, page), 0)
    r = seq % SUBLANES
    if phase == "pages":
        _attn_step_pages(row16, k_refs, v_refs, qcol_ref, m_ref, l_ref, acc_ref, pages=pages, page=page)
        return
    if phase == "last":
        _attn_step_last(r, is_last, kn_ref, vn_ref, lq1_ref, lk1_ref, lq2_ref, lk2_ref, g_ref, o_ref,
                        qblk_ref, m_ref, l_ref, acc_ref, lam_init=lam_init)
        return

    @pl.when(j == 0)
    def _():
        rq = lax.broadcasted_iota(jnp.int32, (nmap, QK_W), 0)
        cq = lax.broadcasted_iota(jnp.int32, (nmap, QK_W), 1)
        qrow = q_ref[pl.ds(r, 1), :]
        qblk_ref[...] = jnp.where(cq // HEAD_DIM_B == rq, jnp.broadcast_to(qrow, (nmap, QK_W)), 0.0)
        qcol_ref[...] = jnp.broadcast_to(qrow, (LANES, QK_W)).T
        m_ref[...] = jnp.full((nmap, 1), -jnp.inf, F32)
        l_ref[...] = jnp.zeros((nmap, 1), F32)
        acc_ref[...] = jnp.zeros((nmap, HEAD_W), F32)


def _attn_step_pages(row16, k_refs, v_refs, qcol_ref, m_ref, l_ref, acc_ref, *, pages, page):
    nmap = 2 * N_HEADS_B
    s_pages = [jnp.zeros((nmap, page), F32) for _ in range(pages)]
    for c in range(nmap):
        cs = slice(c * HEAD_DIM_B, (c + 1) * HEAD_DIM_B)
        q_c = qcol_ref[cs, :]
        for i in range(pages):
            blk = jnp.sum(k_refs[i][0, cs, :] * q_c, axis=0, keepdims=True)
            s_pages[i] = jnp.where(row16 == c, jnp.broadcast_to(blk, (nmap, page)), s_pages[i])
    s = jnp.concatenate(s_pages, axis=1)
    m_old = m_ref[...]
    m_new = jnp.maximum(m_old, jnp.max(s, axis=-1, keepdims=True))
    alpha = jnp.exp(m_old - m_new)
    p = jnp.exp(s - m_new)
    l_ref[...] = alpha * l_ref[...] + jnp.sum(p, axis=-1, keepdims=True)
    m_ref[...] = m_new
    pv = jnp.zeros((nmap, HEAD_W), F32)
    for i in range(pages):
        p_i = p[:, i * page:(i + 1) * page]
        for h in range(N_HEADS_B):
            v_h = v_refs[i][0, pl.ds(h, page, stride=N_HEADS_B), :]
            p_h = jnp.where(row16 // 2 == h, p_i, 0.0).astype(BF16)
            pv = pv + jnp.dot(p_h, v_h.astype(BF16), preferred_element_type=F32)
    acc_ref[...] = alpha * acc_ref[...] + pv


def _attn_step_last(r, is_last, kn_ref, vn_ref, lq1_ref, lk1_ref, lq2_ref, lk2_ref, g_ref, o_ref,
                    qblk_ref, m_ref, l_ref, acc_ref, *, lam_init):
    nmap = 2 * N_HEADS_B

    @pl.when(is_last)
    def _():
        s_new = jnp.sum(qblk_ref[...] * kn_ref[pl.ds(r, 1), :], axis=-1, keepdims=True)
        m_prev = m_ref[...]
        m_fin = jnp.maximum(m_prev, s_new)
        a_fin = jnp.exp(m_prev - m_fin)
        p_new = jnp.exp(s_new - m_fin)
        l = a_fin * l_ref[...] + p_new
        vn = vn_ref[pl.ds(r, 1), :]
        rowh = lax.broadcasted_iota(jnp.int32, (nmap, HEAD_W), 0)
        vn_rows = jnp.zeros((nmap, HEAD_W), F32)
        for h in range(N_HEADS_B):
            vn_rows = jnp.where(rowh // 2 == h,
                                jnp.broadcast_to(vn[:, h * HEAD_W:(h + 1) * HEAD_W], (nmap, HEAD_W)), vn_rows)
        acc = a_fin * acc_ref[...] + p_new * vn_rows
        lam = _lambda(lq1_ref, lk1_ref, lq2_ref, lk2_ref, lam_init)
        scaled = acc * (jnp.where(rowh % 2 == 0, 1.0, -lam) / l)
        outs = [_head_rms(scaled[2 * h:2 * h + 1, :] + scaled[2 * h + 1:2 * h + 2, :], g_ref[...], lam_init)
                for h in range(N_HEADS_B)]
        o_ref[pl.ds(r, 1), :] = jnp.concatenate(outs, axis=1)


N_PROMPT_IN = 8
N_SAMPLE_IN = 4


def _attn_both_kernel(pt_ref, *refs, t, nq, parts, pages, page, lam_init):
    prompt_in = refs[:N_PROMPT_IN]
    sq_ref, kn_ref, vn_ref, grow_ref = refs[N_PROMPT_IN:N_PROMPT_IN + N_SAMPLE_IN]
    base = N_PROMPT_IN + N_SAMPLE_IN
    k_refs = refs[base:base + pages]
    v_refs = refs[base + pages:base + 2 * pages]
    o_ref, so_ref = refs[base + 2 * pages:base + 2 * pages + 2]
    (qs_ref, s0_ref, s1_ref, m_ref, acc_ref,
     qcol_ref, qblk_ref, sm_ref, sl_ref, sacc_ref) = refs[base + 2 * pages + 2:]
    qi = pl.program_id(2)
    step = (pl.program_id(0) * N_HEADS_B + pl.program_id(1)) * nq + qi
    lams = prompt_in[3:7]
    step_part = functools.partial(
        _attn_step_part, seq=step // parts, j=step % parts, is_last=step % parts == parts - 1,
        q_ref=sq_ref, kn_ref=kn_ref, vn_ref=vn_ref, lq1_ref=lams[0], lk1_ref=lams[1], lq2_ref=lams[2], lk2_ref=lams[3],
        g_ref=grow_ref, k_refs=k_refs, v_refs=v_refs, o_ref=so_ref, qcol_ref=qcol_ref, qblk_ref=qblk_ref,
        m_ref=sm_ref, l_ref=sl_ref, acc_ref=sacc_ref, pages=pages, page=page, lam_init=lam_init)
    step_part("first")
    _attn_prompt_tile(qi, functools.partial(step_part, "pages"), *prompt_in, o_ref, qs_ref, s0_ref, s1_ref, m_ref,
                      acc_ref, t=t, lam_init=lam_init)
    step_part("last")


def _attn_both(qb, kb, vtb, sq, k_new, v_new, cache_kt, cache_vf, page_table, lq1, lk1, lq2, lk2, g, lam_init, t):
    bsz, seq, _ = qb.shape
    n = sq.shape[0]
    n_pages = page_table.shape[1]
    page = cache_kt.shape[2]
    nq = seq // t
    steps = bsz * N_HEADS_B * nq
    assert steps % n == 0, "every sample sequence gets the same number of grid steps"
    parts = steps // n
    assert n_pages % parts == 0
    pages = n_pages // parts
    nmap = 2 * N_HEADS_B

    def step_of(b, h, i):
        return (b * N_HEADS_B + h) * nq + i

    vec = pl.BlockSpec((1, HEAD_DIM_B), lambda b, h, i, pt: (0, 0))
    rows = pl.BlockSpec((SUBLANES, QK_W), lambda b, h, i, pt: (step_of(b, h, i) // parts // SUBLANES, 0))

    def page_spec(shape, slot):
        def index(b, h, i, pt):
            s = step_of(b, h, i)
            return (pt[(s // parts) * n_pages + (s % parts) * pages + slot], 0, 0)
        return pl.BlockSpec((1,) + shape, index)

    grid_spec = pltpu.PrefetchScalarGridSpec(
        num_scalar_prefetch=1,
        grid=(bsz, N_HEADS_B, nq),
        in_specs=[pl.BlockSpec((1, t, HEAD_W), lambda b, h, i, pt: (b, i, h)),
                  pl.BlockSpec((1, seq, HEAD_W), lambda b, h, i, pt: (b, 0, h)),
                  pl.BlockSpec((1, 1, nq, VT_ROWS, t), lambda b, h, i, pt: (b, h, 0, 0, 0)),
                  vec, vec, vec, vec,
                  pl.BlockSpec((HEAD_W, 1), lambda b, h, i, pt: (0, 0)),
                  rows, rows, rows,
                  pl.BlockSpec((1, HEAD_W), lambda b, h, i, pt: (0, 0))]
                 + [page_spec((QK_W, page), slot) for slot in range(pages)]
                 + [page_spec((page * N_HEADS_B, HEAD_W), slot) for slot in range(pages)],
        out_specs=[pl.BlockSpec((1, t, HEAD_W), lambda b, h, i, pt: (b, i, h)), rows],
        scratch_shapes=[pltpu.VMEM((2 * t, HEAD_W), BF16), pltpu.VMEM((t, 2 * t), F32), pltpu.VMEM((t, 2 * t), F32),
                        pltpu.VMEM((1, 2 * t), F32), pltpu.VMEM((VT_ROWS, 2 * t), F32),
                        pltpu.VMEM((QK_W, LANES), F32), pltpu.VMEM((nmap, QK_W), F32),
                        pltpu.VMEM((nmap, 1), F32), pltpu.VMEM((nmap, 1), F32), pltpu.VMEM((nmap, HEAD_W), F32)],
    )
    return pl.pallas_call(
        functools.partial(_attn_both_kernel, t=t, nq=nq, parts=parts, pages=pages, page=page, lam_init=lam_init),
        grid_spec=grid_spec,
        out_shape=[jax.ShapeDtypeStruct((bsz, seq, QK_W), BF16), jax.ShapeDtypeStruct((n, QK_W), F32)],
        compiler_params=_cparams(("arbitrary", "arbitrary", "arbitrary")),
        name="attn_both",
    )(page_table.reshape(-1), qb, kb, vtb, lq1, lk1, lq2, lk2, g.reshape(HEAD_W, 1),
      sq, k_new, v_new, g.reshape(1, HEAD_W), *([cache_kt] * pages), *([cache_vf] * pages))


def _lambda_init(layer):
    return 0.8 - 0.6 * math.exp(-0.3 * layer)


def _row_tile(m, want):
    return want if m % want == 0 else m


def _in_proj(x, prm):
    tm = _row_tile(x.shape[0], 1024)
    return _proj(x, prm["w_in"], tm, D_INNER), _proj(x, prm["w_dt"], tm, LANES)


def _layer0_tail(x, y, prm):
    m = x.shape[0]
    x = _proj_ln(y, prm["w_out"], x, prm["ln_g"][0, 0], prm["ln_b"][0, 0], _row_tile(m, 512))
    return _mlp(x, prm["w_up"][0], prm["w_down"][0], prm["ln_g"][0, 1], prm["ln_b"][0, 1], _row_tile(m, 1024), 1024)


def _layer1_tail(x, o, prm):
    m = x.shape[0]
    x = _proj_ln(o, prm["w_o"], x, prm["ln_g"][1, 0], prm["ln_b"][1, 0], _row_tile(m, 512))
    return _mlp(x, prm["w_up"][1], prm["w_down"][1], prm["ln_g"][1, 1], prm["ln_b"][1, 1], _row_tile(m, 1024), 1024)


def kernel(x_prompt, x_sample, state_conv, state_ssm, cache_k, cache_v, page_table, w_in_a, conv_w_a, conv_b_a, dt_bias_a, a_log_a, d_skip_a, norm_a, w_out_a, w_kv, w_q_b, lambda_q1_b, lambda_k1_b, lambda_q2_b, lambda_k2_b, subln_b, w_o_b, w_up, w_down, ln_g, ln_b):
    bp, sp, _ = x_prompt.shape
    bd, ts, _ = x_sample.shape
    assert ts == 1 and sp % ATTN_TILE == 0 and bd % SUBLANES == 0

    w_in = w_in_a[0]
    split = D_INNER + CONV_DIM
    prm = {
        "w_in": w_in[:, :split].astype(BF16),
        "w_dt": jnp.tile(w_in[:, split:], (1, DT_REP)).astype(BF16),
        "w_out": w_out_a[0].astype(BF16),
        "w_kvq": jnp.concatenate([w_kv, w_q_b[0]], axis=1).astype(BF16),
        "w_o": w_o_b[0].astype(BF16),
        "w_up": w_up.astype(BF16),
        "w_down": w_down.astype(BF16),
        "ln_g": ln_g.reshape(DEPTH, 2, 1, D_MODEL),
        "ln_b": ln_b.reshape(DEPTH, 2, 1, D_MODEL),
    }
    cw = conv_w_a[0]
    cb = conv_b_a[0].reshape(1, CONV_DIM)
    dtb = jnp.tile(dt_bias_a[0], DT_REP).reshape(1, LANES)
    alog = jnp.tile(a_log_a[0], DT_REP).reshape(1, LANES)
    dskip = jnp.repeat(d_skip_a[0], HEADDIM_A).reshape(1, D_INNER)
    ng = norm_a[0].reshape(1, D_INNER)
    r128 = _rep_matrix(SSD_CHUNK)
    r64 = _rep_matrix(HEADDIM_A)
    lam_init = _lambda_init(DEPTH // 2)
    lvec = [a[0].reshape(1, HEAD_DIM_B) for a in (lambda_q1_b, lambda_k1_b, lambda_q2_b, lambda_k2_b)]

    xp0 = x_prompt.reshape(bp * sp, D_MODEL)
    xs0 = x_sample.reshape(bd, D_MODEL)
    zx_p, dtr_p = _in_proj(xp0, prm)
    zx_s, dtr_s = _in_proj(xs0, prm)
    zx_p = zx_p.reshape(bp, sp, ZX_W)
    ya_p, ssm_p, ya_s, cst_s, ssm_s = _ssd_both(zx_p, dtr_p.reshape(bp, sp, LANES), zx_s, dtr_s,
                                                state_conv[0].transpose(1, 0, 2), state_ssm[0],
                                                cw, cb, dtb, alog, dskip, ng, r128, r64)
    conv_p = zx_p[:, sp - (CONV_W - 1):, D_INNER:]
    conv_s = cst_s.transpose(1, 0, 2)
    x_p = _layer0_tail(xp0, ya_p.reshape(bp * sp, D_INNER), prm)
    x_s = _layer0_tail(xs0, ya_s, prm)

    kt_p, v_p, kb, vtb, qb = _kvq_prompt(x_p, prm["w_kvq"], bp, sp, ATTN_TILE)
    k_s, v_s, q_s = _kvq_step(x_s, prm["w_kvq"], PAST_LEN)

    n_phys, page = cache_k.shape[:2]
    cache_kt = cache_k.transpose(0, 2, 3, 4, 1).reshape(n_phys, QK_W, page)
    cache_vf = cache_v.reshape(n_phys, page * N_HEADS_B, HEAD_W)
    shp = (bp, sp, QK_W)
    o_p, o_s = _attn_both(qb.reshape(shp), kb.reshape(shp), vtb, q_s, k_s, v_s, cache_kt, cache_vf, page_table,
                          *lvec, subln_b[0], lam_init, ATTN_TILE)

    y_p = _layer1_tail(x_p, o_p.reshape(bp * sp, QK_W), prm)
    y_s = _layer1_tail(x_s, o_s, prm)

    k_p = kt_p.reshape(bp, N_HEADS_B, 2, HEAD_DIM_B, sp).transpose(0, 4, 1, 2, 3)
    return (y_p.reshape(bp, sp, D_MODEL), y_s.reshape(bd, 1, D_MODEL),
            conv_p[None], ssm_p[None], k_p, v_p.reshape(bp, sp, N_HEADS_B, HEAD_W),
            conv_s[None], ssm_s[None],
            k_s.reshape(bd, 1, N_HEADS_B, 2, HEAD_DIM_B), v_s.reshape(bd, 1, N_HEADS_B, HEAD_W))
```

```python
import functools
import math

import jax
import jax.numpy as jnp
from jax import lax
from jax.experimental import pallas as pl
from jax.experimental.pallas import tpu as pltpu

F32 = jnp.float32
BF16 = jnp.bfloat16

D_MODEL = 1024
DEPTH = 2
PAST_LEN = 2048

D_INNER = 2048
HEADDIM_A = 64
N_HEADS_A = 32
D_STATE = 128
N_GROUPS_A = 8
HEADS_PER_GROUP = N_HEADS_A // N_GROUPS_A
GROUP_W = HEADS_PER_GROUP * HEADDIM_A
CONV_W = 4
CONV_DIM = D_INNER + 2 * N_GROUPS_A * D_STATE
SSD_CHUNK = 128
DT_REP = 4
ZX_W = D_INNER + CONV_DIM

HEAD_DIM_B = 64
N_HEADS_B = 8
HEAD_W = 2 * HEAD_DIM_B
QK_W = N_HEADS_B * HEAD_W
ROPE_THETA = 10000.0
ATTN_TILE = 512
VT_ROWS = HEAD_W + 16
Q_SCALE_LOG2 = HEAD_DIM_B ** -0.5 * math.log2(math.e)

D_FF = 4 * D_MODEL
DN_ALPHA = (2 * DEPTH) ** 0.25
LN_EPS = 1e-5
RMS_EPS = 1e-5

LANES = 128
SUBLANES = 8
VMEM_LIMIT = 56 * 1024 * 1024


def _cparams(sem):
    return pltpu.CompilerParams(dimension_semantics=sem, vmem_limit_bytes=VMEM_LIMIT)


def _layer_norm(r, g, b):
    mu = jnp.mean(r, axis=-1, keepdims=True)
    d = r - mu
    var = jnp.mean(d * d, axis=-1, keepdims=True)
    return d * lax.rsqrt(var + LN_EPS) * g + b


def _silu(x):
    h = 0.5 * x
    return h + h * jnp.tanh(h)


def _softplus(x):
    return jnp.maximum(x, 0.0) + jnp.log1p(jnp.exp(-jnp.abs(x)))


def _split3(v):
    hi = v.astype(BF16).astype(F32)
    r1 = v - hi
    mid = r1.astype(BF16).astype(F32)
    lo = r1 - mid
    lane = lax.broadcasted_iota(jnp.int32, v.shape, 1)
    return jnp.where(lane < 32, hi, jnp.where(lane < 64, mid, jnp.where(lane < 96, lo, 0.0))).astype(BF16)


def _rep_matrix(width):
    r = jnp.arange(LANES)[:, None]
    c = jnp.arange(N_HEADS_A * width)[None, :]
    return ((r < 96) & ((r % 32) == (c // width))).astype(BF16)


def _proj_kernel(x_ref, w_ref, o_ref, xb_ref):
    @pl.when(pl.program_id(1) == 0)
    def _():
        xb_ref[...] = x_ref[...].astype(BF16)

    o_ref[...] = jnp.dot(xb_ref[...], w_ref[...], preferred_element_type=F32).astype(o_ref.dtype)


def _proj(x, w, tm, tn):
    m, k = x.shape
    n = w.shape[1]
    return pl.pallas_call(
        _proj_kernel,
        grid=(m // tm, n // tn),
        in_specs=[pl.BlockSpec((tm, k), lambda i, j: (i, 0)),
                  pl.BlockSpec((k, tn), lambda i, j: (0, j))],
        out_specs=pl.BlockSpec((tm, tn), lambda i, j: (i, j)),
        out_shape=jax.ShapeDtypeStruct((m, n), F32),
        scratch_shapes=[pltpu.VMEM((tm, k), BF16)],
        compiler_params=_cparams(("parallel", "arbitrary")),
        name="in_proj",
    )(x, w)


def _proj_ln_kernel(x_ref, w_ref, r_ref, g_ref, b_ref, o_ref):
    y = jnp.dot(x_ref[...].astype(BF16), w_ref[...], preferred_element_type=F32)
    o_ref[...] = _layer_norm(DN_ALPHA * r_ref[...] + y, g_ref[...], b_ref[...])


def _proj_ln(x, w, resid, g, b, tm):
    m, k = x.shape
    n = w.shape[1]
    return pl.pallas_call(
        _proj_ln_kernel,
        grid=(m // tm,),
        in_specs=[pl.BlockSpec((tm, k), lambda i: (i, 0)),
                  pl.BlockSpec((k, n), lambda i: (0, 0)),
                  pl.BlockSpec((tm, n), lambda i: (i, 0)),
                  pl.BlockSpec((1, n), lambda i: (0, 0)),
                  pl.BlockSpec((1, n), lambda i: (0, 0))],
        out_specs=pl.BlockSpec((tm, n), lambda i: (i, 0)),
        out_shape=jax.ShapeDtypeStruct((m, n), F32),
        compiler_params=_cparams(("parallel",)),
        name="proj_ln",
    )(x, w, resid, g, b)


def _mlp_kernel(x_ref, wu_ref, wd_ref, g_ref, b_ref, o_ref, xb_ref, acc_ref):
    j = pl.program_id(1)
    last = pl.num_programs(1) - 1

    def chunk(first, final):
        if first:
            xb_ref[...] = x_ref[...].astype(BF16)
        h = jnp.dot(xb_ref[...], wu_ref[...], preferred_element_type=F32)
        h = jnp.square(jnp.maximum(h, 0.0)).astype(BF16)
        d = jnp.dot(h, wd_ref[...], preferred_element_type=F32)
        acc = d if first else acc_ref[...] + d
        if final:
            o_ref[...] = _layer_norm(DN_ALPHA * x_ref[...] + acc, g_ref[...], b_ref[...])
        else:
            acc_ref[...] = acc

    pl.when(j == 0)(functools.partial(chunk, True, False))
    pl.when(jnp.logical_and(j > 0, j < last))(functools.partial(chunk, False, False))
    pl.when(j == last)(functools.partial(chunk, False, True))


def _mlp(x, w_up, w_down, g, b, tm, tf):
    m, d = x.shape
    ff = w_up.shape[1]
    assert ff // tf >= 2, "the first and the last d_ff chunk are different grid steps"
    return pl.pallas_call(
        _mlp_kernel,
        grid=(m // tm, ff // tf),
        in_specs=[pl.BlockSpec((tm, d), lambda i, j: (i, 0)),
                  pl.BlockSpec((d, tf), lambda i, j: (0, j)),
                  pl.BlockSpec((tf, d), lambda i, j: (j, 0)),
                  pl.BlockSpec((1, d), lambda i, j: (0, 0)),
                  pl.BlockSpec((1, d), lambda i, j: (0, 0))],
        out_specs=pl.BlockSpec((tm, d), lambda i, j: (i, 0)),
        out_shape=jax.ShapeDtypeStruct((m, d), F32),
        scratch_shapes=[pltpu.VMEM((tm, d), BF16), pltpu.VMEM((tm, d), F32)],
        compiler_params=_cparams(("parallel", "arbitrary")),
        name="mlp_ln",
    )(x, w_up, w_down, g, b)


def _shift_rows(cur, prev8, k):
    rolled = pltpu.roll(cur, k, 0)
    rolled_prev = pltpu.roll(prev8, k, 0)
    row = lax.broadcasted_iota(jnp.int32, prev8.shape, 0)
    top = jnp.where(row < k, rolled_prev, rolled[:8])
    return jnp.concatenate([top, rolled[8:]], axis=0)


def _ssd_chunk(c, is_last, alongside, z_ref, x_ref, bc_ref, dt_ref, cw_ref, cb_ref, dtb_ref, alog_ref, dskip_ref,
               ng_ref, r128_ref, r64_ref, y_ref, st_ref, halo_ref, state_ref):
    q = SSD_CHUNK

    @pl.when(c == 0)
    def _():
        halo_ref[...] = jnp.zeros_like(halo_ref)
        state_ref[...] = jnp.zeros_like(state_ref)

    raw = jnp.concatenate([x_ref[0], bc_ref[0]], axis=1)
    prev8 = halo_ref[...]
    conv = cb_ref[...] + cw_ref[CONV_W - 1:CONV_W, :] * raw
    for k in range(1, CONV_W):
        conv = conv + cw_ref[CONV_W - 1 - k:CONV_W - k, :] * _shift_rows(raw, prev8, k)
    halo_ref[...] = raw[q - 8:, :]
    conv = _silu(conv)
    xc = conv[:, :D_INNER]
    bmat = conv[:, D_INNER:D_INNER + N_GROUPS_A * D_STATE]
    cmat = conv[:, D_INNER + N_GROUPS_A * D_STATE:]

    dt = _softplus(dt_ref[0] + dtb_ref[...])
    da = dt * (-jnp.exp(alog_ref[...]))
    ri = lax.broadcasted_iota(jnp.int32, (q, q), 0)
    ci = lax.broadcasted_iota(jnp.int32, (q, q), 1)
    causal = ri >= ci
    tri = causal.astype(BF16)
    hi = da.astype(BF16)
    r1 = da - hi.astype(F32)
    mid = r1.astype(BF16)
    lo = (r1 - mid.astype(F32)).astype(BF16)
    cs = (jnp.dot(tri, hi, preferred_element_type=F32)
          + jnp.dot(tri, mid, preferred_element_type=F32)
          + jnp.dot(tri, lo, preferred_element_type=F32))
    cs_t = cs.T
    cs_last = cs[q - 1:q, :]

    r64 = r64_ref[...]
    dt_rep = jnp.dot(_split3(dt), r64, preferred_element_type=F32)
    e_rep = jnp.dot(_split3(jnp.exp(cs)), r64, preferred_element_type=F32)
    dte_rep = jnp.dot(_split3(jnp.exp(cs_last - cs)), r64, preferred_element_type=F32)
    cs_rep = jnp.dot(_split3(cs), r128_ref[...], preferred_element_type=F32)

    xs = xc * dt_rep
    xs_b = xs.astype(BF16)
    xe_b = (xs * dte_rep).astype(BF16)
    e_last = e_rep[q - 1:q, :]

    y_groups = []
    for g in range(N_GROUPS_A):
        b_g = bmat[:, g * D_STATE:(g + 1) * D_STATE]
        c_g = cmat[:, g * D_STATE:(g + 1) * D_STATE].astype(BF16)
        b_gt = b_g.T.astype(BF16)
        cb = jnp.dot(c_g, b_gt, preferred_element_type=F32)
        gs = slice(g * GROUP_W, (g + 1) * GROUP_W)
        st_g = state_ref[g]
        y_off = jnp.dot(c_g, st_g.astype(BF16), preferred_element_type=F32) * e_rep[:, gs]
        y_heads = []
        for r in range(HEADS_PER_GROUP):
            h = g * HEADS_PER_GROUP + r
            diff = cs_rep[:, h * q:(h + 1) * q] - cs_t[h:h + 1, :]
            w = (cb * jnp.exp(jnp.where(causal, diff, -jnp.inf))).astype(BF16)
            y_heads.append(jnp.dot(w, xs_b[:, h * HEADDIM_A:(h + 1) * HEADDIM_A],
                                   preferred_element_type=F32))
        y_groups.append(jnp.concatenate(y_heads, axis=1) + y_off)
        state_ref[g] = st_g * e_last[:, gs] + jnp.dot(b_gt, xe_b[:, gs], preferred_element_type=F32)

    y = jnp.concatenate(y_groups, axis=1) + dskip_ref[...] * xc
    y = y * _silu(z_ref[0])
    outs = []
    for g in range(N_GROUPS_A):
        yg = y[:, g * GROUP_W:(g + 1) * GROUP_W]
        outs.append(yg * lax.rsqrt(jnp.mean(yg * yg, axis=-1, keepdims=True) + RMS_EPS))
    y_ref[0] = (jnp.concatenate(outs, axis=1) * ng_ref[...]).astype(y_ref.dtype)
    alongside()

    @pl.when(is_last)
    def _():
        for g in range(N_GROUPS_A):
            st_ref[0, g] = state_ref[g].T


def _ssd_step_pre_kernel(dtr_ref, dtb_ref, alog_ref, r64_ref, r128_ref, dtrep_ref, dabc_ref):
    dt = _softplus(dtr_ref[...] + dtb_ref[...])
    da = jnp.exp(dt * (-jnp.exp(alog_ref[...])))
    dtrep_ref[...] = jnp.dot(_split3(dt), r64_ref[...], preferred_element_type=F32)
    dabc_ref[...] = jnp.dot(_split3(da), r128_ref[...], preferred_element_type=F32)


def _ssd_step_pre(dtr, dtb, alog, r64, r128):
    n = dtr.shape[0]
    full = lambda shape: pl.BlockSpec(shape, lambda i: (0,) * len(shape))
    return pl.pallas_call(
        _ssd_step_pre_kernel,
        grid=(1,),
        in_specs=[full((n, LANES)), full((1, LANES)), full((1, LANES)),
                  full((LANES, D_INNER)), full((LANES, N_HEADS_A * LANES))],
        out_specs=[full((n, D_INNER)), full((n, N_HEADS_A * LANES))],
        out_shape=[jax.ShapeDtypeStruct((n, D_INNER), F32), jax.ShapeDtypeStruct((n, N_HEADS_A * LANES), F32)],
        compiler_params=_cparams(("arbitrary",)),
        name="ssd_step_pre",
    )(dtr, dtb, alog, r64, r128)


def _ssd_step_seq(seq, zx_ref, dtrep_ref, dabc_ref, cst_ref, h_ref, cw_ref, cb_ref, dskip_ref, ng_ref,
                  y_ref, cst_out_ref, h_out_ref):
    r = seq % SUBLANES
    zx = zx_ref[pl.ds(r, 1), :]
    raw = zx[:, D_INNER:D_INNER + CONV_DIM]
    conv = cb_ref[...] + cw_ref[CONV_W - 1:CONV_W, :] * raw
    for k in range(CONV_W - 1):
        prev = cst_ref[k, pl.ds(r, 1), :]
        conv = conv + cw_ref[k:k + 1, :] * prev
        if k > 0:
            cst_out_ref[k - 1, pl.ds(r, 1), :] = prev
    cst_out_ref[CONV_W - 2, pl.ds(r, 1), :] = raw
    conv = _silu(conv)
    xc = conv[:, :D_INNER]
    bmat = conv[:, D_INNER:D_INNER + N_GROUPS_A * D_STATE]
    cmat = conv[:, D_INNER + N_GROUPS_A * D_STATE:]

    xdt = xc * dtrep_ref[pl.ds(r, 1), :]
    da = dabc_ref[pl.ds(r, 1), :]

    xdt_col = jnp.broadcast_to(xdt, (LANES, D_INNER)).T

    row8 = lax.broadcasted_iota(jnp.int32, (8, D_STATE), 0)
    c_rows = jnp.zeros((8, D_STATE), F32)
    for g in range(N_GROUPS_A):
        c_rows = jnp.where(row8 == g, jnp.broadcast_to(cmat[:, g * D_STATE:(g + 1) * D_STATE], (8, D_STATE)), c_rows)
    c_rows = c_rows.astype(BF16)

    y_rows = []
    for g in range(N_GROUPS_A):
        gs = slice(g * GROUP_W, (g + 1) * GROUP_W)
        b_g = bmat[:, g * D_STATE:(g + 1) * D_STATE]
        heads = []
        for hh in range(g * HEADS_PER_GROUP, (g + 1) * HEADS_PER_GROUP):
            hs = slice(hh * HEADDIM_A, (hh + 1) * HEADDIM_A)
            heads.append(h_ref[0, hs, :] * da[:, hh * LANES:(hh + 1) * LANES] + xdt_col[hs, :] * b_g)
        h_new = jnp.concatenate(heads, axis=0)
        h_out_ref[0, gs, :] = h_new
        yg = lax.dot_general(c_rows, h_new.astype(BF16), (((1,), (1,)), ((), ())),
                             preferred_element_type=F32)
        y_rows.append(yg[g:g + 1, :])
    y = jnp.concatenate(y_rows, axis=1) + dskip_ref[...] * xc
    y = y * _silu(zx[:, :D_INNER])
    outs = []
    for g in range(N_GROUPS_A):
        yg = y[:, g * GROUP_W:(g + 1) * GROUP_W]
        outs.append(yg * lax.rsqrt(jnp.mean(yg * yg, axis=-1, keepdims=True) + RMS_EPS))
    y_ref[pl.ds(r, 1), :] = jnp.concatenate(outs, axis=1) * ng_ref[...]


N_CHUNK_IN = 12
N_SEQ_IN = 5


def _ssd_both_kernel(*refs, nc):
    chunk_in = refs[:N_CHUNK_IN]
    seq_in = refs[N_CHUNK_IN:N_CHUNK_IN + N_SEQ_IN]
    outs = refs[N_CHUNK_IN + N_SEQ_IN:N_CHUNK_IN + N_SEQ_IN + 5]
    halo_ref, state_ref = refs[N_CHUNK_IN + N_SEQ_IN + 5:]
    c = pl.program_id(1)
    cw_ref, cb_ref, dskip_ref, ng_ref = chunk_in[4], chunk_in[5], chunk_in[8], chunk_in[9]
    step_seq = functools.partial(_ssd_step_seq, pl.program_id(0) * nc + c, *seq_in, cw_ref, cb_ref, dskip_ref, ng_ref,
                                 *outs[2:])
    _ssd_chunk(c, c == nc - 1, step_seq, *chunk_in, outs[0], outs[1], halo_ref, state_ref)


def _ssd_both(zx, dtr, zx_s, dtr_s, conv_state_t, ssm_state, cw, cb, dtb, alog, dskip, ng, r128, r64):
    bsz, seq, _ = zx.shape
    n = zx_s.shape[0]
    q = SSD_CHUNK
    nc = seq // q
    assert bsz * nc == n, "one sample sequence per prompt chunk step"
    dt_rep, da_bc = _ssd_step_pre(dtr_s, dtb, alog, r64, r128)
    const = lambda shape: pl.BlockSpec(shape, lambda b, c: (0,) * len(shape))
    rows = lambda w: pl.BlockSpec((SUBLANES, w), lambda b, c: ((b * nc + c) // SUBLANES, 0))
    cst_blk = pl.BlockSpec((CONV_W - 1, SUBLANES, CONV_DIM), lambda b, c: (0, (b * nc + c) // SUBLANES, 0))
    h_blk = pl.BlockSpec((1, D_INNER, D_STATE), lambda b, c: (b * nc + c, 0, 0))
    y, st, y_s, cst, h = pl.pallas_call(
        functools.partial(_ssd_both_kernel, nc=nc),
        grid=(bsz, nc),
        in_specs=[pl.BlockSpec((1, q, D_INNER), lambda b, c: (b, c, 0)),
                  pl.BlockSpec((1, q, D_INNER), lambda b, c: (b, c, 1)),
                  pl.BlockSpec((1, q, D_INNER), lambda b, c: (b, c, 2)),
                  pl.BlockSpec((1, q, LANES), lambda b, c: (b, c, 0)),
                  const((CONV_W, CONV_DIM)), const((1, CONV_DIM)), const((1, LANES)), const((1, LANES)),
                  const((1, D_INNER)), const((1, D_INNER)),
                  const((LANES, N_HEADS_A * q)), const((LANES, D_INNER)),
                  rows(ZX_W), rows(D_INNER), rows(N_HEADS_A * LANES), cst_blk, h_blk],
        out_specs=[pl.BlockSpec((1, q, D_INNER), lambda b, c: (b, c, 0)),
                   pl.BlockSpec((1, N_GROUPS_A, GROUP_W, D_STATE), lambda b, c: (b, 0, 0, 0)),
                   rows(D_INNER), cst_blk, h_blk],
        out_shape=[jax.ShapeDtypeStruct((bsz, seq, D_INNER), BF16),
                   jax.ShapeDtypeStruct((bsz, N_GROUPS_A, GROUP_W, D_STATE), F32),
                   jax.ShapeDtypeStruct((n, D_INNER), F32),
                   jax.ShapeDtypeStruct((CONV_W - 1, n, CONV_DIM), F32),
                   jax.ShapeDtypeStruct((n, D_INNER, D_STATE), F32)],
        scratch_shapes=[pltpu.VMEM((8, CONV_DIM), F32),
                        pltpu.VMEM((N_GROUPS_A, D_STATE, GROUP_W), F32)],
        compiler_params=_cparams(("arbitrary", "arbitrary")),
        name="ssd_both",
    )(zx, zx, zx, dtr, cw, cb, dtb, alog, dskip, ng, r128, r64,
      zx_s, dt_rep, da_bc, conv_state_t, ssm_state.reshape(n, D_INNER, D_STATE))
    return (y, st.reshape(bsz, N_HEADS_A, HEADDIM_A, D_STATE),
            y_s, cst, h.reshape(n, N_HEADS_A, HEADDIM_A, D_STATE))


def _rope_angles(pos_col):
    lane = lax.broadcasted_iota(jnp.int32, pos_col.shape, 1)
    half = HEAD_DIM_B // 2
    idx = (lane % half).astype(F32)
    inv = jnp.exp(idx * (-math.log(ROPE_THETA) / half))
    ang = pos_col * inv
    return jnp.cos(ang), jnp.sin(ang)


def _rope_signed(sin):
    lane = lax.broadcasted_iota(jnp.int32, sin.shape, 1)
    return jnp.where((lane % HEAD_DIM_B) < HEAD_DIM_B // 2, -sin, sin)


def _rope(x, cos, sin_signed):
    n = x.shape[1]
    reps = n // LANES
    cos_f = jnp.concatenate([cos] * reps, axis=1)
    sin_f = jnp.concatenate([sin_signed] * reps, axis=1)
    half = HEAD_DIM_B // 2
    first = (lax.broadcasted_iota(jnp.int32, x.shape, 1) % HEAD_DIM_B) < half
    rot = jnp.where(first, pltpu.roll(x, n - half, 1), pltpu.roll(x, half, 1))
    return x * cos_f + rot * sin_f


def _kvq_values(x_ref, w_ref, cos, sin_signed):
    xb = x_ref[...].astype(BF16)
    k = _rope(jnp.dot(xb, w_ref[:, :QK_W], preferred_element_type=F32), cos, sin_signed)
    v = jnp.dot(xb, w_ref[:, QK_W:2 * QK_W], preferred_element_type=F32)
    q = _rope(jnp.dot(xb, w_ref[:, 2 * QK_W:], preferred_element_type=F32), cos, sin_signed)
    return k, v, q


def _kvq_prompt_kernel(x_ref, w_ref, kt_ref, v_ref, kb_ref, vt_ref, qb_ref, cos_in_ref, sin_in_ref, *, tm, seq):
    i = pl.program_id(0)

    @pl.when(i == 0)
    def _():
        within = lax.broadcasted_iota(jnp.int32, (tm, LANES), 0).astype(F32)
        cos_in_ref[...], sin_in_ref[...] = _rope_angles(within)

    start = jnp.full((SUBLANES, LANES), (i * tm) % seq, jnp.int32).astype(F32)
    cos_st, sin_st = _rope_angles(start)
    cos_st, sin_st = cos_st[0:1], sin_st[0:1]
    cos = cos_st * cos_in_ref[...] - sin_st * sin_in_ref[...]
    sin = sin_st * cos_in_ref[...] + cos_st * sin_in_ref[...]
    k, v, q = _kvq_values(x_ref, w_ref, cos, _rope_signed(sin))
    kt_ref[0] = k.T
    kb_ref[...] = k.astype(BF16)
    v_ref[...] = v
    vt_ref[0, :, 0, :HEAD_W, :] = v.T.reshape(N_HEADS_B, HEAD_W, tm).astype(BF16)
    vt_ref[0, :, 0, HEAD_W:, :] = jnp.ones((N_HEADS_B, VT_ROWS - HEAD_W, tm), BF16)
    qb_ref[...] = (q * Q_SCALE_LOG2).astype(BF16)


def _kvq_prompt(x, w_kvq, bsz, seq, tm):
    m = x.shape[0]
    nblk = seq // tm
    blk = pl.BlockSpec((tm, QK_W), lambda i: (i, 0))
    return pl.pallas_call(
        functools.partial(_kvq_prompt_kernel, tm=tm, seq=seq),
        grid=(m // tm,),
        in_specs=[pl.BlockSpec((tm, D_MODEL), lambda i: (i, 0)),
                  pl.BlockSpec((D_MODEL, 3 * QK_W), lambda i: (0, 0))],
        out_specs=[pl.BlockSpec((1, QK_W, tm), lambda i: (i // nblk, 0, i % nblk)),
                   blk, blk,
                   pl.BlockSpec((1, N_HEADS_B, 1, VT_ROWS, tm), lambda i: (i // nblk, 0, i % nblk, 0, 0)),
                   blk],
        out_shape=[jax.ShapeDtypeStruct((bsz, QK_W, seq), F32), jax.ShapeDtypeStruct((m, QK_W), F32),
                   jax.ShapeDtypeStruct((m, QK_W), BF16),
                   jax.ShapeDtypeStruct((bsz, N_HEADS_B, nblk, VT_ROWS, tm), BF16),
                   jax.ShapeDtypeStruct((m, QK_W), BF16)],
        scratch_shapes=[pltpu.VMEM((tm, LANES), F32), pltpu.VMEM((tm, LANES), F32)],
        compiler_params=_cparams(("arbitrary",)),
        name="kvq_prompt",
    )(x, w_kvq)


def _kvq_step_kernel(x_ref, w_ref, k_ref, v_ref, q_ref, *, tm, pos0):
    cos, sin = _rope_angles(jnp.full((tm, LANES), pos0, F32))
    k, v, q = _kvq_values(x_ref, w_ref, cos, _rope_signed(sin))
    k_ref[...] = k
    v_ref[...] = v
    q_ref[...] = q * (HEAD_DIM_B ** -0.5)


def _kvq_step(x, w_kvq, pos0):
    m = x.shape[0]
    blk = pl.BlockSpec((m, QK_W), lambda i: (0, 0))
    return pl.pallas_call(
        functools.partial(_kvq_step_kernel, tm=m, pos0=pos0),
        grid=(1,),
        in_specs=[pl.BlockSpec((m, D_MODEL), lambda i: (0, 0)),
                  pl.BlockSpec((D_MODEL, 3 * QK_W), lambda i: (0, 0))],
        out_specs=[blk, blk, blk],
        out_shape=[jax.ShapeDtypeStruct((m, QK_W), F32)] * 3,
        compiler_params=_cparams(("arbitrary",)),
        name="kvq_step",
    )(x, w_kvq)


def _lambda(lq1_ref, lk1_ref, lq2_ref, lk2_ref, lam_init):
    s1 = jnp.sum(lq1_ref[...] * lk1_ref[...], axis=-1, keepdims=True)
    s2 = jnp.sum(lq2_ref[...] * lk2_ref[...], axis=-1, keepdims=True)
    return jnp.exp(s1) - jnp.exp(s2) + lam_init


def _head_rms(o, g, lam_init):
    return o * lax.rsqrt(jnp.mean(o * o, axis=-1, keepdims=True) + RMS_EPS) * g * (1.0 - lam_init)


def _attn_prompt_tile(qi, alongside, q_ref, k_ref, vt_ref, lq1_ref, lk1_ref, lq2_ref, lk2_ref, g_ref, o_ref,
                      qs_ref, s0_ref, s1_ref, m_ref, acc_ref, *, t, lam_init):
    q = q_ref[0].astype(F32)
    lane = lax.broadcasted_iota(jnp.int32, (t, HEAD_W), 1)
    qs_ref[...] = jnp.concatenate([jnp.where(lane < HEAD_DIM_B, q, 0.0),
                                   jnp.where(lane >= HEAD_DIM_B, q, 0.0)], axis=0).astype(BF16)
    m_ref[...] = jnp.full_like(m_ref, -jnp.inf)
    acc_ref[...] = jnp.zeros_like(acc_ref)

    def scores(ki, s_ref):
        start = pl.multiple_of(ki * t, t)
        k = k_ref[0, pl.ds(start, t), :]
        s_ref[...] = lax.dot_general(k, qs_ref[...], (((1,), (1,)), ((), ())),
                                     preferred_element_type=F32)

    def update(ki, s_ref, masked):
        s = s_ref[...]
        if masked:
            key = lax.broadcasted_iota(jnp.int32, (t, 2 * t), 0)
            qry = lax.broadcasted_iota(jnp.int32, (t, 2 * t), 1) % t
            s = jnp.where(key <= qry, s, -jnp.inf)
        m_old = m_ref[...]
        m_new = jnp.maximum(m_old, jnp.max(s, axis=0, keepdims=True))
        p = jnp.exp2(s - m_new).astype(BF16)
        vt = vt_ref[0, 0, ki]
        acc_ref[...] = jnp.exp2(m_old - m_new) * acc_ref[...] + jnp.dot(vt, p, preferred_element_type=F32)
        m_ref[...] = m_new

    scores(0, s0_ref)
    alongside()

    def pair(p, carry):
        scores(2 * p + 1, s1_ref)
        update(2 * p, s0_ref, False)
        scores(2 * p + 2, s0_ref)
        update(2 * p + 1, s1_ref, False)
        return carry

    lax.fori_loop(0, qi // 2, pair, 0)

    @pl.when(qi % 2 == 0)
    def _():
        update(qi, s0_ref, True)

    @pl.when(qi % 2 == 1)
    def _():
        scores(qi, s1_ref)
        update(qi - 1, s0_ref, False)
        update(qi, s1_ref, True)

    lam = _lambda(lq1_ref, lk1_ref, lq2_ref, lk2_ref, lam_init)
    acc = acc_ref[:HEAD_W, :]
    l = acc_ref[HEAD_W:HEAD_W + 1, :]
    o = acc[:, :t] / l[:, :t] - lam * (acc[:, t:] / l[:, t:])
    o = o * lax.rsqrt(jnp.mean(o * o, axis=0, keepdims=True) + RMS_EPS) * g_ref[...] * (1.0 - lam_init)
    o_ref[0] = o.T.astype(o_ref.dtype)


def _attn_step_part(phase, seq, j, is_last, q_ref, kn_ref, vn_ref, lq1_ref, lk1_ref, lq2_ref, lk2_ref, g_ref,
                    k_refs, v_refs, o_ref, qcol_ref, qblk_ref, m_ref, l_ref, acc_ref, *, pages, page, lam_init):
    nmap = 2 * N_HEADS_B
    row16 = lax.broadcasted_iota(jnp.int32, (nmap, page), 0)
    r = seq % SUBLANES
    if phase == "pages":
        _attn_step_pages(row16, k_refs, v_refs, qcol_ref, m_ref, l_ref, acc_ref, pages=pages, page=page)
        return
    if phase == "last":
        _attn_step_last(r, is_last, kn_ref, vn_ref, lq1_ref, lk1_ref, lq2_ref, lk2_ref, g_ref, o_ref,
                        qblk_ref, m_ref, l_ref, acc_ref, lam_init=lam_init)
        return

    @pl.when(j == 0)
    def _():
        rq = lax.broadcasted_iota(jnp.int32, (nmap, QK_W), 0)
        cq = lax.broadcasted_iota(jnp.int32, (nmap, QK_W), 1)
        qrow = q_ref[pl.ds(r, 1), :]
        qblk_ref[...] = jnp.where(cq // HEAD_DIM_B == rq, jnp.broadcast_to(qrow, (nmap, QK_W)), 0.0)
        qcol_ref[...] = jnp.broadcast_to(qrow, (LANES, QK_W)).T
        m_ref[...] = jnp.full((nmap, 1), -jnp.inf, F32)
        l_ref[...] = jnp.zeros((nmap, 1), F32)
        acc_ref[...] = jnp.zeros((nmap, HEAD_W), F32)


def _attn_step_pages(row16, k_refs, v_refs, qcol_ref, m_ref, l_ref, acc_ref, *, pages, page):
    nmap = 2 * N_HEADS_B
    s_pages = [jnp.zeros((nmap, page), F32) for _ in range(pages)]
    for c in range(nmap):
        cs = slice(c * HEAD_DIM_B, (c + 1) * HEAD_DIM_B)
        q_c = qcol_ref[cs, :]
        for i in range(pages):
            blk = jnp.sum(k_refs[i][0, cs, :] * q_c, axis=0, keepdims=True)
            s_pages[i] = jnp.where(row16 == c, jnp.broadcast_to(blk, (nmap, page)), s_pages[i])
    s = jnp.concatenate(s_pages, axis=1)
    m_old = m_ref[...]
    m_new = jnp.maximum(m_old, jnp.max(s, axis=-1, keepdims=True))
    alpha = jnp.exp(m_old - m_new)
    p = jnp.exp(s - m_new)
    l_ref[...] = alpha * l_ref[...] + jnp.sum(p, axis=-1, keepdims=True)
    m_ref[...] = m_new
    pv = jnp.zeros((nmap, HEAD_W), F32)
    for i in range(pages):
        p_i = p[:, i * page:(i + 1) * page]
        for h in range(N_HEADS_B):
            v_h = v_refs[i][0, pl.ds(h, page, stride=N_HEADS_B), :]
            p_h = jnp.where(row16 // 2 == h, p_i, 0.0).astype(BF16)
            pv = pv + jnp.dot(p_h, v_h.astype(BF16), preferred_element_type=F32)
    acc_ref[...] = alpha * acc_ref[...] + pv


def _attn_step_last(r, is_last, kn_ref, vn_ref, lq1_ref, lk1_ref, lq2_ref, lk2_ref, g_ref, o_ref,
                    qblk_ref, m_ref, l_ref, acc_ref, *, lam_init):
    nmap = 2 * N_HEADS_B

    @pl.when(is_last)
    def _():
        s_new = jnp.sum(qblk_ref[...] * kn_ref[pl.ds(r, 1), :], axis=-1, keepdims=True)
        m_prev = m_ref[...]
        m_fin = jnp.maximum(m_prev, s_new)
        a_fin = jnp.exp(m_prev - m_fin)
        p_new = jnp.exp(s_new - m_fin)
        l = a_fin * l_ref[...] + p_new
        vn = vn_ref[pl.ds(r, 1), :]
        rowh = lax.broadcasted_iota(jnp.int32, (nmap, HEAD_W), 0)
        vn_rows = jnp.zeros((nmap, HEAD_W), F32)
        for h in range(N_HEADS_B):
            vn_rows = jnp.where(rowh // 2 == h,
                                jnp.broadcast_to(vn[:, h * HEAD_W:(h + 1) * HEAD_W], (nmap, HEAD_W)), vn_rows)
        acc = a_fin * acc_ref[...] + p_new * vn_rows
        lam = _lambda(lq1_ref, lk1_ref, lq2_ref, lk2_ref, lam_init)
        scaled = acc * (jnp.where(rowh % 2 == 0, 1.0, -lam) / l)
        outs = [_head_rms(scaled[2 * h:2 * h + 1, :] + scaled[2 * h + 1:2 * h + 2, :], g_ref[...], lam_init)
                for h in range(N_HEADS_B)]
        o_ref[pl.ds(r, 1), :] = jnp.concatenate(outs, axis=1)


N_PROMPT_IN = 8
N_SAMPLE_IN = 4


def _attn_both_kernel(pt_ref, *refs, t, nq, parts, pages, page, lam_init):
    prompt_in = refs[:N_PROMPT_IN]
    sq_ref, kn_ref, vn_ref, grow_ref = refs[N_PROMPT_IN:N_PROMPT_IN + N_SAMPLE_IN]
    base = N_PROMPT_IN + N_SAMPLE_IN
    k_refs = refs[base:base + pages]
    v_refs = refs[base + pages:base + 2 * pages]
    o_ref, so_ref = refs[base + 2 * pages:base + 2 * pages + 2]
    (qs_ref, s0_ref, s1_ref, m_ref, acc_ref,
     qcol_ref, qblk_ref, sm_ref, sl_ref, sacc_ref) = refs[base + 2 * pages + 2:]
    qi = pl.program_id(2)
    step = (pl.program_id(0) * N_HEADS_B + pl.program_id(1)) * nq + qi
    lams = prompt_in[3:7]
    step_part = functools.partial(
        _attn_step_part, seq=step // parts, j=step % parts, is_last=step % parts == parts - 1,
        q_ref=sq_ref, kn_ref=kn_ref, vn_ref=vn_ref, lq1_ref=lams[0], lk1_ref=lams[1], lq2_ref=lams[2], lk2_ref=lams[3],
        g_ref=grow_ref, k_refs=k_refs, v_refs=v_refs, o_ref=so_ref, qcol_ref=qcol_ref, qblk_ref=qblk_ref,
        m_ref=sm_ref, l_ref=sl_ref, acc_ref=sacc_ref, pages=pages, page=page, lam_init=lam_init)
    step_part("first")
    _attn_prompt_tile(qi, functools.partial(step_part, "pages"), *prompt_in, o_ref, qs_ref, s0_ref, s1_ref, m_ref,
                      acc_ref, t=t, lam_init=lam_init)
    step_part("last")


def _attn_both(qb, kb, vtb, sq, k_new, v_new, cache_kt, cache_vf, page_table, lq1, lk1, lq2, lk2, g, lam_init, t):
    bsz, seq, _ = qb.shape
    n = sq.shape[0]
    n_pages = page_table.shape[1]
    page = cache_kt.shape[2]
    nq = seq // t
    steps = bsz * N_HEADS_B * nq
    assert steps % n == 0, "every sample sequence gets the same number of grid steps"
    parts = steps // n
    assert n_pages % parts == 0
    pages = n_pages // parts
    nmap = 2 * N_HEADS_B

    def step_of(b, h, i):
        return (b * N_HEADS_B + h) * nq + i

    vec = pl.BlockSpec((1, HEAD_DIM_B), lambda b, h, i, pt: (0, 0))
    rows = pl.BlockSpec((SUBLANES, QK_W), lambda b, h, i, pt: (step_of(b, h, i) // parts // SUBLANES, 0))

    def page_spec(shape, slot):
        def index(b, h, i, pt):
            s = step_of(b, h, i)
            return (pt[(s // parts) * n_pages + (s % parts) * pages + slot], 0, 0)
        return pl.BlockSpec((1,) + shape, index)

    grid_spec = pltpu.PrefetchScalarGridSpec(
        num_scalar_prefetch=1,
        grid=(bsz, N_HEADS_B, nq),
        in_specs=[pl.BlockSpec((1, t, HEAD_W), lambda b, h, i, pt: (b, i, h)),
                  pl.BlockSpec((1, seq, HEAD_W), lambda b, h, i, pt: (b, 0, h)),
                  pl.BlockSpec((1, 1, nq, VT_ROWS, t), lambda b, h, i, pt: (b, h, 0, 0, 0)),
                  vec, vec, vec, vec,
                  pl.BlockSpec((HEAD_W, 1), lambda b, h, i, pt: (0, 0)),
                  rows, rows, rows,
                  pl.BlockSpec((1, HEAD_W), lambda b, h, i, pt: (0, 0))]
                 + [page_spec((QK_W, page), slot) for slot in range(pages)]
                 + [page_spec((page * N_HEADS_B, HEAD_W), slot) for slot in range(pages)],
        out_specs=[pl.BlockSpec((1, t, HEAD_W), lambda b, h, i, pt: (b, i, h)), rows],
        scratch_shapes=[pltpu.VMEM((2 * t, HEAD_W), BF16), pltpu.VMEM((t, 2 * t), F32), pltpu.VMEM((t, 2 * t), F32),
                        pltpu.VMEM((1, 2 * t), F32), pltpu.VMEM((VT_ROWS, 2 * t), F32),
                        pltpu.VMEM((QK_W, LANES), F32), pltpu.VMEM((nmap, QK_W), F32),
                        pltpu.VMEM((nmap, 1), F32), pltpu.VMEM((nmap, 1), F32), pltpu.VMEM((nmap, HEAD_W), F32)],
    )
    return pl.pallas_call(
        functools.partial(_attn_both_kernel, t=t, nq=nq, parts=parts, pages=pages, page=page, lam_init=lam_init),
        grid_spec=grid_spec,
        out_shape=[jax.ShapeDtypeStruct((bsz, seq, QK_W), BF16), jax.ShapeDtypeStruct((n, QK_W), F32)],
        compiler_params=_cparams(("arbitrary", "arbitrary", "arbitrary")),
        name="attn_both",
    )(page_table.reshape(-1), qb, kb, vtb, lq1, lk1, lq2, lk2, g.reshape(HEAD_W, 1),
      sq, k_new, v_new, g.reshape(1, HEAD_W), *([cache_kt] * pages), *([cache_vf] * pages))


def _lambda_init(layer):
    return 0.8 - 0.6 * math.exp(-0.3 * layer)


def _row_tile(m, want):
    return want if m % want == 0 else m


def _in_proj(x, prm):
    tm = _row_tile(x.shape[0], 1024)
    return _proj(x, prm["w_in"], tm, D_INNER), _proj(x, prm["w_dt"], tm, LANES)


def _layer0_tail(x, y, prm):
    m = x.shape[0]
    x = _proj_ln(y, prm["w_out"], x, prm["ln_g"][0, 0], prm["ln_b"][0, 0], _row_tile(m, 512))
    return _mlp(x, prm["w_up"][0], prm["w_down"][0], prm["ln_g"][0, 1], prm["ln_b"][0, 1], _row_tile(m, 1024), 1024)


def _layer1_tail(x, o, prm):
    m = x.shape[0]
    x = _proj_ln(o, prm["w_o"], x, prm["ln_g"][1, 0], prm["ln_b"][1, 0], _row_tile(m, 512))
    return _mlp(x, prm["w_up"][1], prm["w_down"][1], prm["ln_g"][1, 1], prm["ln_b"][1, 1], _row_tile(m, 1024), 1024)


def kernel(x_prompt, x_sample, state_conv, state_ssm, cache_k, cache_v, page_table, w_in_a, conv_w_a, conv_b_a, dt_bias_a, a_log_a, d_skip_a, norm_a, w_out_a, w_kv, w_q_b, lambda_q1_b, lambda_k1_b, lambda_q2_b, lambda_k2_b, subln_b, w_o_b, w_up, w_down, ln_g, ln_b):
    bp, sp, _ = x_prompt.shape
    bd, ts, _ = x_sample.shape
    assert ts == 1 and sp % ATTN_TILE == 0 and bd % SUBLANES == 0

    w_in = w_in_a[0]
    split = D_INNER + CONV_DIM
    prm = {
        "w_in": w_in[:, :split].astype(BF16),
        "w_dt": jnp.tile(w_in[:, split:], (1, DT_REP)).astype(BF16),
        "w_out": w_out_a[0].astype(BF16),
        "w_kvq": jnp.concatenate([w_kv, w_q_b[0]], axis=1).astype(BF16),
        "w_o": w_o_b[0].astype(BF16),
        "w_up": w_up.astype(BF16),
        "w_down": w_down.astype(BF16),
        "ln_g": ln_g.reshape(DEPTH, 2, 1, D_MODEL),
        "ln_b": ln_b.reshape(DEPTH, 2, 1, D_MODEL),
    }
    cw = conv_w_a[0]
    cb = conv_b_a[0].reshape(1, CONV_DIM)
    dtb = jnp.tile(dt_bias_a[0], DT_REP).reshape(1, LANES)
    alog = jnp.tile(a_log_a[0], DT_REP).reshape(1, LANES)
    dskip = jnp.repeat(d_skip_a[0], HEADDIM_A).reshape(1, D_INNER)
    ng = norm_a[0].reshape(1, D_INNER)
    r128 = _rep_matrix(SSD_CHUNK)
    r64 = _rep_matrix(HEADDIM_A)
    lam_init = _lambda_init(DEPTH // 2)
    lvec = [a[0].reshape(1, HEAD_DIM_B) for a in (lambda_q1_b, lambda_k1_b, lambda_q2_b, lambda_k2_b)]

    xp0 = x_prompt.reshape(bp * sp, D_MODEL)
    xs0 = x_sample.reshape(bd, D_MODEL)
    zx_p, dtr_p = _in_proj(xp0, prm)
    zx_s, dtr_s = _in_proj(xs0, prm)
    zx_p = zx_p.reshape(bp, sp, ZX_W)
    ya_p, ssm_p, ya_s, cst_s, ssm_s = _ssd_both(zx_p, dtr_p.reshape(bp, sp, LANES), zx_s, dtr_s,
                                                state_conv[0].transpose(1, 0, 2), state_ssm[0],
                                                cw, cb, dtb, alog, dskip, ng, r128, r64)
    conv_p = zx_p[:, sp - (CONV_W - 1):, D_INNER:]
    conv_s = cst_s.transpose(1, 0, 2)
    x_p = _layer0_tail(xp0, ya_p.reshape(bp * sp, D_INNER), prm)
    x_s = _layer0_tail(xs0, ya_s, prm)

    kt_p, v_p, kb, vtb, qb = _kvq_prompt(x_p, prm["w_kvq"], bp, sp, ATTN_TILE)
    k_s, v_s, q_s = _kvq_step(x_s, prm["w_kvq"], PAST_LEN)

    n_phys, page = cache_k.shape[:2]
    cache_kt = cache_k.transpose(0, 2, 3, 4, 1).reshape(n_phys, QK_W, page)
    cache_vf = cache_v.reshape(n_phys, page * N_HEADS_B, HEAD_W)
    shp = (bp, sp, QK_W)
    o_p, o_s = _attn_both(qb.reshape(shp), kb.reshape(shp), vtb, q_s, k_s, v_s, cache_kt, cache_vf, page_table,
                          *lvec, subln_b[0], lam_init, ATTN_TILE)

    y_p = _layer1_tail(x_p, o_p.reshape(bp * sp, QK_W), prm)
    y_s = _layer1_tail(x_s, o_s, prm)

    k_p = kt_p.reshape(bp, N_HEADS_B, 2, HEAD_DIM_B, sp).transpose(0, 4, 1, 2, 3)
    return (y_p.reshape(bp, sp, D_MODEL), y_s.reshape(bd, 1, D_MODEL),
            conv_p[None], ssm_p[None], k_p, v_p.reshape(bp, sp, N_HEADS_B, HEAD_W),
            conv_s[None], ssm_s[None],
            k_s.reshape(bd, 1, N_HEADS_B, 2, HEAD_DIM_B), v_s.reshape(bd, 1, N_HEADS_B, HEAD_W))
```
